```python
import jax, jax.numpy as jnp
from jax import lax
import numpy as np

D_MODEL = 1024
BATCH = 16
SEQ = 4096
DEPTH = 2

CHUNK = 64
CONV_WIDTH = 4
A_HEADS = 4
A_DK = 64
A_DV = 64
B_HEADS = 4
B_DK = 48
B_DV = 96
GLA_RANK = 16
GLA_GATE_TEMP = 16.0
C_HEADS = 6
C_DK = 64
C_DV = 64

A_QK = A_HEADS * A_DK
A_V = A_HEADS * A_DV
B_QK = B_HEADS * B_DK
B_V = B_HEADS * B_DV
C_QK = C_HEADS * C_DK
C_V = C_HEADS * C_DV
C_QKV = 2 * C_QK + C_V
D_MIX = A_V + B_V + C_V
SPLIT_SIZES = (A_QK, A_QK, A_V, A_V, B_QK, B_QK, B_V, GLA_RANK, B_V, C_QKV, C_HEADS, C_HEADS, C_V)
D_IN = A_QK * 2 + A_V * 2 + B_QK * 2 + B_V * 2 + GLA_RANK + C_QKV + 2 * C_HEADS + C_V

DEEPNORM_ALPHA = (2 * DEPTH) ** 0.25
DEEPNORM_BETA = (8 * DEPTH) ** -0.25
LN_EPS = 1e-5
NORM_EPS = 1e-6
FORGET_FLOOR = 1e-30

kernel_name = 'hybrid_hgrn2_gla_gdn_streaming_layer'


def _layer_norm(x):
    x32 = x.astype(jnp.float32)
    mu = jnp.mean(x32, axis=-1, keepdims=True)
    var = jnp.mean(jnp.square(x32 - mu), axis=-1, keepdims=True)
    return (x32 - mu) * lax.rsqrt(var + LN_EPS)


def _head_rmsnorm(o, gain):
    return o * lax.rsqrt(jnp.mean(o * o, axis=-1, keepdims=True) + NORM_EPS) * gain.astype(jnp.float32)


def _l2norm(t):
    return t * lax.rsqrt(jnp.sum(t * t, axis=-1, keepdims=True) + NORM_EPS)


def _causal_depthwise_conv(x, w):
    s = x.shape[1]
    xp = jnp.pad(x, ((0, 0), (CONV_WIDTH - 1, 0), (0, 0)))
    return sum(xp[:, j:j + s] * w[j] for j in range(CONV_WIDTH))


def _chunk(t):
    b, s, h, d = t.shape
    return t.reshape(b, s // CHUNK, CHUNK, h, d).transpose(1, 0, 3, 2, 4)


def _unchunk(t):
    n, b, h, c, d = t.shape
    return t.transpose(1, 0, 3, 2, 4).reshape(b, n * c, h, d)


def _masked_decay(rel, mask):
    return jnp.where(mask, jnp.exp(jnp.where(mask, rel, 0.0)), 0.0)


def _gla_chunked(q, k, v, log_g):
    qc, kc, vc, gc = _chunk(q), _chunk(k), _chunk(v), _chunk(log_g)
    causal = jnp.tril(jnp.ones((CHUNK, CHUNK), dtype=bool))[:, :, None]

    def step(state, xs):
        qi, ki, vi, gi = xs
        bcum = jnp.cumsum(gi, axis=-2)
        rel = bcum[..., :, None, :] - bcum[..., None, :, :]
        decay = _masked_decay(rel, causal)
        scores = jnp.sum(qi[..., :, None, :] * ki[..., None, :, :] * decay, axis=-1)
        o = (jnp.einsum('bhts,bhsv->bhtv', scores, vi)
             + jnp.einsum('bhtk,bhkv->bhtv', qi * jnp.exp(bcum), state))
        b_last = bcum[..., -1:, :]
        state = (jnp.exp(b_last[..., 0, :])[..., None] * state
                 + jnp.einsum('bhsk,bhsv->bhkv', ki * jnp.exp(b_last - bcum), vi))
        return state, o

    bsz, _, h, dk = q.shape
    state0 = jnp.zeros((bsz, h, dk, v.shape[-1]), jnp.float32)
    _, o = lax.scan(step, state0, (qc, kc, vc, gc))
    return _unchunk(o)


def _gated_delta_chunked(q, k, v, log_a, beta):
    qc, kc, vc = _chunk(q), _chunk(k), _chunk(v)
    gc = _chunk(log_a[..., None])[..., 0]
    bc = _chunk(beta[..., None])[..., 0]
    bcum = jnp.cumsum(gc, axis=-1)
    causal = jnp.tril(jnp.ones((CHUNK, CHUNK), dtype=bool))
    strict = jnp.tril(jnp.ones((CHUNK, CHUNK), dtype=bool), k=-1)
    rel = bcum[..., :, None] - bcum[..., None, :]
    decay = _masked_decay(rel, causal)
    k_beta = kc * bc[..., None]
    v_beta = vc * bc[..., None]
    lower = jnp.where(strict, jnp.einsum('nbhtk,nbhsk->nbhts', k_beta, kc) * decay, 0.0)
    eye = jnp.eye(CHUNK, dtype=lower.dtype)
    t_inv = lax.linalg.triangular_solve(eye + lower, jnp.broadcast_to(eye, lower.shape),
                                        left_side=True, lower=True, unit_diagonal=True)
    u = jnp.einsum('nbhts,nbhsv->nbhtv', t_inv, v_beta)
    w = jnp.einsum('nbhts,nbhsk->nbhtk', t_inv, k_beta * jnp.exp(bcum)[..., None])
    attn = jnp.where(causal, jnp.einsum('nbhtk,nbhsk->nbhts', qc, kc) * decay, 0.0)

    def step(state, xs):
        qi, ki, ui, wi, bi, ai = xs
        v_new = ui - jnp.einsum('bhtk,bhkv->bhtv', wi, state)
        o = (jnp.einsum('bhtk,bhkv->bhtv', qi * jnp.exp(bi)[..., None], state)
             + jnp.einsum('bhts,bhsv->bhtv', ai, v_new))
        b_last = bi[..., -1:]
        state = (jnp.exp(b_last)[..., None] * state
                 + jnp.einsum('bhsk,bhsv->bhkv', ki * jnp.exp(b_last - bi)[..., None], v_new))
        return state, o

    bsz, _, h, dk = q.shape
    state0 = jnp.zeros((bsz, h, dk, v.shape[-1]), jnp.float32)
    _, o = lax.scan(step, state0, (qc, kc, u, w, bcum, attn))
    return _unchunk(o)


def _hybrid_layer(x, c_act, lb, w_in, w_out, ada_w, ada_b, ln_g, ln_b, gla_w_gk, gla_b_gk,
                  gdn_conv_w, gdn_a_log, gdn_dt_bias, gain_a, gain_b, gain_c):
    bsz, s, _ = x.shape
    f32 = jnp.float32
    mod = (c_act @ ada_w + ada_b).astype(f32)
    shift, scale, gate = jnp.split(mod, 3, axis=-1)
    h = _layer_norm(x) * (1 + scale[:, None, :]) + shift[:, None, :]
    proj = (h.astype(x.dtype) @ w_in).astype(f32)
    offsets = [int(o) for o in np.cumsum(SPLIT_SIZES)[:-1]]
    qa, fa, ia, za, qb, kb, vb, lrb, zb, qkvc, ac, bc, zc = jnp.split(proj, offsets, axis=-1)

    def heads(t, n):
        return t.reshape(bsz, s, n, -1)

    fa_logit = heads(fa, A_HEADS)
    lb_h = lb.astype(f32).reshape(A_HEADS, A_DK)
    f_a = lb_h + (1 - lb_h) * jax.nn.sigmoid(fa_logit)
    log_f = jnp.log(jnp.maximum(f_a, FORGET_FLOOR))
    k_a = (1 - lb_h) * jax.nn.sigmoid(-fa_logit)
    o_a = _gla_chunked(jax.nn.silu(heads(qa, A_HEADS)), k_a, heads(ia, A_HEADS), log_f)

    gk = jax.nn.log_sigmoid(lrb @ gla_w_gk.astype(f32) + gla_b_gk.astype(f32)) / GLA_GATE_TEMP
    o_b = _gla_chunked(heads(qb, B_HEADS) * (B_DK ** -0.5), heads(kb, B_HEADS),
                       heads(vb, B_HEADS), heads(gk, B_HEADS))

    qkvc = jax.nn.silu(_causal_depthwise_conv(qkvc, gdn_conv_w.astype(f32)))
    qc, kc, vc = jnp.split(qkvc, [C_QK, 2 * C_QK], axis=-1)
    qc = _l2norm(heads(qc, C_HEADS)) * (C_DK ** -0.5)
    kc = _l2norm(heads(kc, C_HEADS))
    log_a = -jnp.exp(gdn_a_log.astype(f32)) * jax.nn.softplus(ac + gdn_dt_bias.astype(f32))
    beta = jax.nn.sigmoid(bc)
    o_c = _gated_delta_chunked(qc, kc, heads(vc, C_HEADS), log_a, beta)

    y = jnp.concatenate([
        _head_rmsnorm(o_a, gain_a).reshape(bsz, s, A_V) * jax.nn.silu(za),
        _head_rmsnorm(o_b, gain_b).reshape(bsz, s, B_V) * jax.nn.silu(zb),
        _head_rmsnorm(o_c, gain_c).reshape(bsz, s, C_V) * jax.nn.silu(zc),
    ], axis=-1)
    out = (y.astype(x.dtype) @ w_out).astype(f32)
    res = DEEPNORM_ALPHA * x.astype(f32) + gate[:, None, :] * out
    return (_layer_norm(res) * ln_g.astype(f32) + ln_b.astype(f32)).astype(x.dtype)


def setup_inputs(seed: int = 0) -> dict:
    key = jax.random.key(seed)
    ks = jax.random.split(key, 18)
    d = D_MODEL
    x = jax.random.normal(ks[0], (BATCH, SEQ, d), jnp.float32)
    c = jax.random.normal(ks[1], (BATCH, d), jnp.float32)
    b_ = DEEPNORM_BETA
    seg = [(A_QK, 1.0), (A_QK, 1.0), (A_V, b_), (A_V, 1.0),
           (B_QK, 1.0), (B_QK, 1.0), (B_V, b_), (GLA_RANK, 1.0), (B_V, 1.0),
           (2 * C_QK, 1.0), (C_V, b_), (C_HEADS, 1.0), (C_HEADS, 1.0), (C_V, 1.0)]
    col_scale = jnp.concatenate([jnp.full((n,), sc, jnp.float32) for n, sc in seg])
    w_in = jax.random.normal(ks[2], (DEPTH, d, D_IN), jnp.float32) * (d ** -0.5) * col_scale
    w_out = jax.random.normal(ks[3], (DEPTH, D_MIX, d), jnp.float32) * (D_MIX ** -0.5) * DEEPNORM_BETA
    ada_w = jax.random.normal(ks[4], (DEPTH, d, 3 * d), jnp.float32) * (0.5 * d ** -0.5)
    ada_b = jax.random.normal(ks[5], (DEPTH, 3 * d), jnp.float32) * 0.02
    ln_g = 1.0 + 0.02 * jax.random.normal(ks[6], (DEPTH, d), jnp.float32)
    ln_b = 0.02 * jax.random.normal(ks[7], (DEPTH, d), jnp.float32)
    hgrn_lb_logits = 0.5 * jax.random.normal(ks[8], (DEPTH, A_QK), jnp.float32)
    gla_w_gk = jax.random.normal(ks[9], (DEPTH, GLA_RANK, B_QK), jnp.float32) * (GLA_RANK ** -0.5)
    gla_b_gk = 0.1 * jax.random.normal(ks[10], (DEPTH, B_QK), jnp.float32)
    gdn_conv_w = jax.random.normal(ks[11], (DEPTH, CONV_WIDTH, C_QKV), jnp.float32) * (CONV_WIDTH ** -0.5)
    gdn_a_log = jnp.log(jax.random.uniform(ks[12], (DEPTH, C_HEADS), jnp.float32, 1.0, 16.0))
    dt = jnp.exp(jax.random.uniform(ks[13], (DEPTH, C_HEADS), jnp.float32, float(np.log(1e-3)), float(np.log(1e-1))))
    gdn_dt_bias = dt + jnp.log(-jnp.expm1(-dt))
    gain_a = 1.0 + 0.02 * jax.random.normal(ks[14], (DEPTH, A_DV), jnp.float32)
    gain_b = 1.0 + 0.02 * jax.random.normal(ks[15], (DEPTH, B_DV), jnp.float32)
    gain_c = 1.0 + 0.02 * jax.random.normal(ks[16], (DEPTH, C_DV), jnp.float32)
    return {'x': x, 'c': c, 'w_in': w_in, 'w_out': w_out, 'ada_w': ada_w, 'ada_b': ada_b,
            'ln_g': ln_g, 'ln_b': ln_b, 'hgrn_lb_logits': hgrn_lb_logits,
            'gla_w_gk': gla_w_gk, 'gla_b_gk': gla_b_gk, 'gdn_conv_w': gdn_conv_w,
            'gdn_a_log': gdn_a_log, 'gdn_dt_bias': gdn_dt_bias,
            'gain_a': gain_a, 'gain_b': gain_b, 'gain_c': gain_c}


def reference(x, c, w_in, w_out, ada_w, ada_b, ln_g, ln_b, hgrn_lb_logits, gla_w_gk, gla_b_gk,
              gdn_conv_w, gdn_a_log, gdn_dt_bias, gain_a, gain_b, gain_c):
    c_act = jax.nn.silu(c)
    p = jax.nn.softmax(hgrn_lb_logits.astype(jnp.float32), axis=0)
    lb_table = jnp.cumsum(p, axis=0) - p[0:1]
    for l in range(DEPTH):
        x = _hybrid_layer(x, c_act, lb_table[l], w_in[l], w_out[l], ada_w[l], ada_b[l],
                          ln_g[l], ln_b[l], gla_w_gk[l], gla_b_gk[l], gdn_conv_w[l],
                          gdn_a_log[l], gdn_dt_bias[l], gain_a[l], gain_b[l], gain_c[l])
    return x
```

```python
import functools

import numpy as np
import jax
import jax.numpy as jnp
from jax import lax
from jax.experimental import pallas as pl
from jax.experimental.pallas import tpu as pltpu

F32 = jnp.float32
BF16 = jnp.bfloat16

D_MODEL = 1024
DEPTH = 2
CHUNK = 64
CONV_WIDTH = 4
A_HEADS, A_DK, A_DV = 4, 64, 64
B_HEADS, B_DK, B_DV = 4, 48, 96
GLA_RANK = 16
GLA_GATE_TEMP = 16.0
C_HEADS, C_DK, C_DV = 6, 64, 64
DEEPNORM_ALPHA = (2 * DEPTH) ** 0.25
LN_EPS = 1e-5
NORM_EPS = 1e-6
FORGET_FLOOR = 1e-30

A_QK = A_HEADS * A_DK
A_V = A_HEADS * A_DV
B_QK = B_HEADS * B_DK
B_V = B_HEADS * B_DV
C_QK = C_HEADS * C_DK
C_V = C_HEADS * C_DV
C_QKV = 2 * C_QK + C_V
SPLIT_SIZES = (A_QK, A_QK, A_V, A_V, B_QK, B_QK, B_V, GLA_RANK, B_V, C_QKV, C_HEADS, C_HEADS, C_V)
D_IN = sum(SPLIT_SIZES)

LANES = 128
SUBLANES = 8
VMEM_LIMIT_BYTES = 56 * 1024 * 1024

B_DKP = 64
B_DVP = 128
B_QKP = B_HEADS * B_DKP
B_VP = B_HEADS * B_DVP
SUB = 16
NSUB = CHUNK // SUB

OFF_QA = 0
OFF_FA = OFF_QA + A_QK
OFF_IA = OFF_FA + A_QK
OFF_ZA = OFF_IA + A_V
OFF_QB = OFF_ZA + A_V
OFF_KB = OFF_QB + B_QKP
OFF_VB = OFF_KB + B_QKP
OFF_ZB = OFF_VB + B_VP
OFF_QKVC = OFF_ZB + B_VP
OFF_ZC = OFF_QKVC + C_QKV
OFF_MISC = OFF_ZC + C_V
D_INP = OFF_MISC + LANES
MISC_LR = 0
MISC_AC = MISC_LR + GLA_RANK
MISC_BC = MISC_AC + C_HEADS
D_MIXP = A_V + B_VP + C_V
OFF_YA, OFF_YB, OFF_YC = 0, A_V, A_V + B_VP


def _padded_column_sources():
    offs = np.concatenate([[0], np.cumsum(SPLIT_SIZES)])
    (o_qa, o_fa, o_ia, o_za, o_qb, o_kb, o_vb, o_lr, o_zb, o_qkvc, o_ac, o_bc, o_zc) = offs[:-1]
    src = np.full((D_INP,), -1, np.int64)
    for dst, s0 in ((OFF_QA, o_qa), (OFF_FA, o_fa), (OFF_IA, o_ia), (OFF_ZA, o_za)):
        src[dst:dst + A_QK] = np.arange(s0, s0 + A_QK)
    for h in range(B_HEADS):
        src[OFF_QB + h * B_DKP:OFF_QB + h * B_DKP + B_DK] = np.arange(o_qb + h * B_DK, o_qb + (h + 1) * B_DK)
        src[OFF_KB + h * B_DKP:OFF_KB + h * B_DKP + B_DK] = np.arange(o_kb + h * B_DK, o_kb + (h + 1) * B_DK)
        src[OFF_VB + h * B_DVP:OFF_VB + h * B_DVP + B_DV] = np.arange(o_vb + h * B_DV, o_vb + (h + 1) * B_DV)
        src[OFF_ZB + h * B_DVP:OFF_ZB + h * B_DVP + B_DV] = np.arange(o_zb + h * B_DV, o_zb + (h + 1) * B_DV)
    src[OFF_QKVC:OFF_QKVC + C_QKV] = np.arange(o_qkvc, o_qkvc + C_QKV)
    src[OFF_ZC:OFF_ZC + C_V] = np.arange(o_zc, o_zc + C_V)
    src[OFF_MISC + MISC_LR:OFF_MISC + MISC_LR + GLA_RANK] = np.arange(o_lr, o_lr + GLA_RANK)
    src[OFF_MISC + MISC_AC:OFF_MISC + MISC_AC + C_HEADS] = np.arange(o_ac, o_ac + C_HEADS)
    src[OFF_MISC + MISC_BC:OFF_MISC + MISC_BC + C_HEADS] = np.arange(o_bc, o_bc + C_HEADS)
    return src


def _padded_mix_sources():
    src = np.full((D_MIXP,), -1, np.int64)
    src[OFF_YA:OFF_YA + A_V] = np.arange(0, A_V)
    for h in range(B_HEADS):
        src[OFF_YB + h * B_DVP:OFF_YB + h * B_DVP + B_DV] = np.arange(A_V + h * B_DV, A_V + (h + 1) * B_DV)
    src[OFF_YC:OFF_YC + C_V] = np.arange(A_V + B_V, A_V + B_V + C_V)
    return src


def _gather_padded(w, src, axis):
    idx = jnp.asarray(np.maximum(src, 0), jnp.int32)
    valid = jnp.asarray(src >= 0)
    shape = [1] * w.ndim
    shape[axis] = src.shape[0]
    return jnp.where(valid.reshape(shape), jnp.take(w, idx, axis=axis), 0.0)


def _block_ones(rows, rgroup, cols, cgroup):
    r = np.arange(rows)[:, None] // rgroup
    c = np.arange(cols)[None, :] // cgroup
    return (r == c).astype(np.float32)


def _head_select(nheads, width, group):
    m = np.zeros((max(nheads, SUBLANES), width), np.float32)
    for h in range(nheads):
        m[h, h * group:(h + 1) * group] = 1.0
    return m


def _level_masks():
    t = np.arange(CHUNK)[:, None]
    s = np.arange(CHUNK)[None, :]
    masks = []
    m = 1
    while m < CHUNK:
        masks.append(((t // (2 * m) == s // (2 * m)) & (t % (2 * m) >= m) & (s % (2 * m) < m)).astype(np.float32))
        m *= 2
    return np.stack(masks)


def _constants():
    ltri = np.tril(np.ones((CHUNK, CHUNK), np.float32))
    e_a = np.zeros((LANES, C_QK), np.float32)
    e_b = np.zeros((LANES, C_QK), np.float32)
    for h in range(C_HEADS):
        e_a[MISC_AC + h, h * C_DK:(h + 1) * C_DK] = 1.0
        e_b[MISC_BC + h, h * C_DK:(h + 1) * C_DK] = 1.0
    return dict(
        ltri=jnp.asarray(ltri, BF16),
        e_a=jnp.asarray(e_a, BF16),
        e_b=jnp.asarray(e_b, BF16),
        ones_a=jnp.asarray(_block_ones(A_QK, A_DK, A_V, A_DV), BF16),
        ones_b=jnp.asarray(_block_ones(B_QKP, B_DKP, B_VP, B_DVP), BF16),
        ones_bv=jnp.asarray(_block_ones(B_VP, B_DVP, B_VP, B_DVP), BF16),
        ones_c=jnp.asarray(_block_ones(C_QK, C_DK, C_QK, C_DK), BF16),
        mask_a=jnp.asarray(_block_ones(A_V, A_DV, A_QK, A_DK), F32),
        mask_b=jnp.asarray(_block_ones(B_VP, B_DVP, B_QKP, B_DKP), F32),
        hmk=jnp.asarray(_head_select(A_HEADS, A_QK, A_DK), F32),
        hmv_a=jnp.asarray(_head_select(A_HEADS, A_V, A_DV), F32),
        hmv_b=jnp.asarray(_head_select(B_HEADS, B_VP, B_DVP), F32),
        lvl=jnp.asarray(_level_masks(), F32),
    )


def _sigmoid(x):
    return 1.0 / (1.0 + jnp.exp(-x))


def _softplus(x):
    return jnp.maximum(x, 0.0) + jnp.log1p(jnp.exp(-jnp.abs(x)))


def _split_bf16(a):
    hi = a.astype(BF16)
    lo = (a - hi.astype(F32)).astype(BF16)
    return hi, lo


def _dot(a, b):
    return jnp.dot(a, b, preferred_element_type=F32)


def _dot_nt(a, b):
    return lax.dot_general(a, b, (((1,), (1,)), ((), ())), preferred_element_type=F32)


def _dot_tn(a, b):
    return lax.dot_general(a, b, (((0,), (0,)), ((), ())), preferred_element_type=F32)


def _dot_lhs2(a, b_bf16):
    hi, lo = _split_bf16(a)
    return _dot(hi, b_bf16) + _dot(lo, b_bf16)


def _dot_rhs2(a_bf16, b):
    hi, lo = _split_bf16(b)
    return _dot(a_bf16, hi) + _dot(a_bf16, lo)


def _dot3(a, b):
    ah, al = _split_bf16(a)
    bh, bl = _split_bf16(b)
    return _dot(ah, bh) + (_dot(ah, bl) + _dot(al, bh))


def _layer_norm(x):
    mu = jnp.mean(x, axis=-1, keepdims=True)
    xc = x - mu
    var = jnp.mean(xc * xc, axis=-1, keepdims=True)
    return xc * lax.rsqrt(var + LN_EPS)


def _gla_chunk(q, k, v, bcum, st_ref, ones_kv, st_mask, hmk, hmv, nheads):
    st = st_ref[...]
    b_last = bcum[CHUNK - 1:CHUNK, :]
    o_inter = _dot_nt((q * jnp.exp(bcum)).astype(BF16), st.astype(BF16))
    v16 = v.astype(BF16)
    tio = lax.broadcasted_iota(jnp.int32, (SUB, q.shape[1]), 0)
    outs = []
    for i in range(NSUB):
        n = i * SUB
        qi = q[n:n + SUB]
        ki = k[n:n + SUB]
        bi = bcum[n:n + SUB]
        vi = v[n:n + SUB]
        xs = []
        for s in range(SUB):
            rel = jnp.minimum(bi - bi[s:s + 1], 0.0)
            x = qi * ki[s:s + 1] * jnp.exp(rel)
            xs.append(jnp.where(tio >= s, x, 0.0))
        x_all = jnp.concatenate(xs, axis=0).astype(BF16)
        r_all = _dot(x_all, ones_kv)
        o_i = r_all[0:SUB] * vi[0:1]
        for s in range(1, SUB):
            o_i = o_i + r_all[s * SUB:(s + 1) * SUB] * vi[s:s + 1]
        if i > 0:
            ri = bcum[n:n + 1]
            ksc = (k[0:n] * jnp.exp(ri - bcum[0:n])).astype(BF16)
            qt = qi * jnp.exp(bi - ri)
            qstack = jnp.concatenate([qt * hmk[h:h + 1] for h in range(nheads)], axis=0).astype(BF16)
            sst = _dot_nt(qstack, ksc)
            r_od = _dot(sst.astype(BF16), v16[0:n])
            for h in range(nheads):
                o_i = o_i + r_od[h * SUB:(h + 1) * SUB] * hmv[h:h + 1]
        outs.append(o_i)
    o = jnp.concatenate(outs, axis=0) + o_inter
    kdec = (k * jnp.exp(b_last - bcum)).astype(BF16)
    st_ref[...] = st * jnp.exp(b_last) + _dot_tn(v16, kdec) * st_mask
    return o


def _unit_lower_inverse(nmat, lvl_ref):
    r = lax.broadcasted_iota(jnp.int32, (CHUNK, CHUNK), 0)
    c = lax.broadcasted_iota(jnp.int32, (CHUNK, CHUNK), 1)
    tinv = jnp.where(r == c, 1.0, 0.0) - nmat * lvl_ref[0]
    for lev in range(1, lvl_ref.shape[0]):
        nc = nmat * lvl_ref[lev]
        tinv = tinv - _dot3(tinv, _dot3(nc, tinv))
    return tinv


def _layer_kernel(x_ref, mod_ref, win_ref, wout_ref, lng_ref, lnb_ref, lb_ref, wgkh_ref, wgkl_ref, bgk_ref,
                  convw_ref, avec_ref, dtvec_ref, gain_ref,
                  ltri_ref, ea_ref, eb_ref, ones_a_ref, ones_b_ref, ones_bv_ref, ones_c_ref,
                  mask_a_ref, mask_b_ref, hmk_ref, hmva_ref, hmvb_ref, lvl_ref,
                  out_ref,
                  proj_ref, y_ref, sa_ref, sb_ref, sc_ref, conv_ref):
    tile = x_ref.shape[1]

    @pl.when(pl.program_id(1) == 0)
    def _():
        sa_ref[...] = jnp.zeros_like(sa_ref)
        sb_ref[...] = jnp.zeros_like(sb_ref)
        sc_ref[...] = jnp.zeros_like(sc_ref)
        conv_ref[0:SUBLANES, :] = jnp.zeros((SUBLANES, C_QKV), F32)

    x = x_ref[0]
    shift = mod_ref[0, 0:1, :]
    scale = mod_ref[0, 1:2, :]
    gate = mod_ref[0, 2:3, :]
    h = _layer_norm(x) * (1.0 + scale) + shift
    proj_ref[...] = _dot(h.astype(BF16), win_ref[...])

    ltri = ltri_ref[...]
    hmk = hmk_ref[...]
    hmva = hmva_ref[...]
    hmvb = hmvb_ref[...]
    t_io = lax.broadcasted_iota(jnp.int32, (CHUNK, CHUNK), 0)
    s_io = lax.broadcasted_iota(jnp.int32, (CHUNK, CHUNK), 1)
    causal = t_io >= s_io
    strict = t_io > s_io

    def chunk_body(c, carry):
        rows = pl.ds(pl.multiple_of(c * CHUNK, CHUNK), CHUNK)

        qa = proj_ref[rows, OFF_QA:OFF_QA + A_QK]
        fa = proj_ref[rows, OFF_FA:OFF_FA + A_QK]
        ia = proj_ref[rows, OFF_IA:OFF_IA + A_V]
        lbv = lb_ref[...]
        sg = _sigmoid(fa)
        f_a = lbv + (1.0 - lbv) * sg
        g_a = jnp.log(jnp.maximum(f_a, FORGET_FLOOR))
        k_a = (1.0 - lbv) * (1.0 - sg)
        q_a = qa * _sigmoid(qa)
        o_a = _gla_chunk(q_a, k_a, ia, _dot_rhs2(ltri, g_a), sa_ref, ones_a_ref[...], mask_a_ref[...],
                         hmk, hmva, A_HEADS)

        misc = proj_ref[rows, OFF_MISC:OFF_MISC + LANES]
        m_hi, m_lo = _split_bf16(misc)
        gk_lin = (_dot(m_hi, wgkh_ref[...]) + (_dot(m_lo, wgkh_ref[...]) + _dot(m_hi, wgkl_ref[...]))
                  + bgk_ref[...])
        g_b = -_softplus(-gk_lin) * (1.0 / GLA_GATE_TEMP)
        q_b = proj_ref[rows, OFF_QB:OFF_QB + B_QKP] * (B_DK ** -0.5)
        k_b = proj_ref[rows, OFF_KB:OFF_KB + B_QKP]
        v_b = proj_ref[rows, OFF_VB:OFF_VB + B_VP]
        o_b = _gla_chunk(q_b, k_b, v_b, _dot_rhs2(ltri, g_b), sb_ref, ones_b_ref[...], mask_b_ref[...],
                         hmk, hmvb, B_HEADS)

        raw = proj_ref[rows, OFF_QKVC:OFF_QKVC + C_QKV]
        conv_ref[SUBLANES:SUBLANES + CHUNK, :] = raw
        cw = convw_ref[...]
        acc = raw * cw[CONV_WIDTH - 1:CONV_WIDTH]
        for j in range(CONV_WIDTH - 1):
            lo = SUBLANES - (CONV_WIDTH - 1) + j
            acc = acc + conv_ref[lo:lo + CHUNK, :] * cw[j:j + 1]
        conv_ref[0:SUBLANES, :] = conv_ref[CHUNK:CHUNK + SUBLANES, :]
        qkv = acc * _sigmoid(acc)
        q_c = qkv[:, 0:C_QK]
        k_c = qkv[:, C_QK:2 * C_QK]
        v_c = qkv[:, 2 * C_QK:C_QKV]
        ones_c = ones_c_ref[...]
        q_c = q_c * lax.rsqrt(_dot_lhs2(q_c * q_c, ones_c) + NORM_EPS) * (C_DK ** -0.5)
        k_c = k_c * lax.rsqrt(_dot_lhs2(k_c * k_c, ones_c) + NORM_EPS)
        log_a = -jnp.exp(avec_ref[...]) * _softplus(misc + dtvec_ref[...])
        beta = _sigmoid(misc)
        bc_small = _dot_rhs2(ltri, log_a)
        bexp = _dot_lhs2(bc_small, ea_ref[...])
        beta_x = _dot_lhs2(beta, eb_ref[...])
        b_rows = bc_small.T
        e_b = jnp.exp(bexp)
        bl_c = bexp[CHUNK - 1:CHUNK, :]
        e_bl = jnp.exp(bl_c - bexp)
        e_last = jnp.exp(bl_c)
        kb_c = k_c * beta_x
        vb_c = v_c * beta_x
        kbe_c = kb_c * e_b
        qe_c = q_c * e_b
        kdl_c = k_c * e_bl
        o_heads = []
        for hd in range(C_HEADS):
            sl = slice(hd * C_DK, (hd + 1) * C_DK)
            rel = bexp[:, sl] - b_rows[MISC_AC + hd:MISC_AC + hd + 1, :]
            dec = jnp.exp(jnp.minimum(rel, 0.0))
            kh16 = k_c[:, sl].astype(BF16)
            kk = _dot_nt(kb_c[:, sl].astype(BF16), kh16)
            tinv = _unit_lower_inverse(jnp.where(strict, kk * dec, 0.0), lvl_ref)
            t16 = tinv.astype(BF16)
            u = _dot(t16, vb_c[:, sl].astype(BF16))
            w = _dot(t16, kbe_c[:, sl].astype(BF16))
            attn = jnp.where(causal, _dot_nt(q_c[:, sl].astype(BF16), kh16) * dec, 0.0)
            s_h = sc_ref[hd]
            s16 = s_h.astype(BF16)
            v_new = u - _dot(w.astype(BF16), s16)
            vn16 = v_new.astype(BF16)
            o_heads.append(_dot(qe_c[:, sl].astype(BF16), s16) + _dot(attn.astype(BF16), vn16))
            sc_ref[hd] = s_h * e_last[:, sl] + _dot_tn(kdl_c[:, sl].astype(BF16), vn16)
        o_c = jnp.concatenate(o_heads, axis=1)

        def norm_gate(o, z, ones, dv, gain):
            ms = _dot_lhs2(o * o, ones) * (1.0 / dv)
            return (o * lax.rsqrt(ms + NORM_EPS) * gain * (z * _sigmoid(z))).astype(BF16)

        za = proj_ref[rows, OFF_ZA:OFF_ZA + A_V]
        zb = proj_ref[rows, OFF_ZB:OFF_ZB + B_VP]
        zc = proj_ref[rows, OFF_ZC:OFF_ZC + C_V]
        y_ref[rows, OFF_YA:OFF_YA + A_V] = norm_gate(o_a, za, ones_a_ref[...], A_DV, gain_ref[:, OFF_YA:OFF_YA + A_V])
        y_ref[rows, OFF_YB:OFF_YB + B_VP] = norm_gate(o_b, zb, ones_bv_ref[...], B_DV, gain_ref[:, OFF_YB:OFF_YB + B_VP])
        y_ref[rows, OFF_YC:OFF_YC + C_V] = norm_gate(o_c, zc, ones_c, C_DV, gain_ref[:, OFF_YC:OFF_YC + C_V])
        return carry

    lax.fori_loop(0, tile // CHUNK, chunk_body, 0)

    out = _dot(y_ref[...], wout_ref[...])
    res = DEEPNORM_ALPHA * x + gate * out
    out_ref[0] = _layer_norm(res) * lng_ref[...] + lnb_ref[...]


def _seq_tile(seq):
    tile = CHUNK
    while tile * 2 <= min(seq, 256) and seq % (tile * 2) == 0:
        tile *= 2
    return tile


def _const_spec(arr):
    nd = arr.ndim
    return pl.BlockSpec(arr.shape, lambda b, j, _nd=nd: (0,) * _nd)


def _hybrid_layer_call(x, mod, w_in_p, w_out_p, ln_g, ln_b, lbv, wgk_hi, wgk_lo, bgk, conv_w, avec, dtvec, gains,
                       consts):
    bsz, seq, d = x.shape
    assert d == D_MODEL and seq % CHUNK == 0
    tile = _seq_tile(seq)
    names = ("ltri", "e_a", "e_b", "ones_a", "ones_b", "ones_bv", "ones_c", "mask_a", "mask_b",
             "hmk", "hmv_a", "hmv_b", "lvl")
    small = (w_in_p, w_out_p, ln_g, ln_b, lbv, wgk_hi, wgk_lo, bgk, conv_w, avec, dtvec, gains) + tuple(
        consts[n] for n in names)
    in_specs = [
        pl.BlockSpec((1, tile, D_MODEL), lambda b, j: (b, j, 0)),
        pl.BlockSpec((1, 3, D_MODEL), lambda b, j: (b, 0, 0)),
    ] + [_const_spec(a) for a in small]
    return pl.pallas_call(
        _layer_kernel,
        grid=(bsz, seq // tile),
        in_specs=in_specs,
        out_specs=pl.BlockSpec((1, tile, D_MODEL), lambda b, j: (b, j, 0)),
        out_shape=jax.ShapeDtypeStruct((bsz, seq, D_MODEL), F32),
        scratch_shapes=[
            pltpu.VMEM((tile, D_INP), F32),
            pltpu.VMEM((tile, D_MIXP), BF16),
            pltpu.VMEM((A_V, A_QK), F32),
            pltpu.VMEM((B_VP, B_QKP), F32),
            pltpu.VMEM((C_HEADS, C_DK, C_DV), F32),
            pltpu.VMEM((SUBLANES + CHUNK, C_QKV), F32),
        ],
        compiler_params=pltpu.CompilerParams(
            dimension_semantics=("arbitrary", "arbitrary"),
            vmem_limit_bytes=VMEM_LIMIT_BYTES),
        name="hybrid_layer",
    )(x, mod, *small)


def _mod_kernel(c_ref, w_ref, b_ref, o_ref):
    c = c_ref[...]
    c_act = c * _sigmoid(c)
    o_ref[0] = _dot3(c_act, w_ref[0]) + b_ref[0]


def _ada_mod_call(c, ada_w, ada_b):
    depth, d, d3 = ada_w.shape
    bsz = c.shape[0]
    ncol = 512
    assert d3 % ncol == 0
    return pl.pallas_call(
        _mod_kernel,
        grid=(depth, d3 // ncol),
        in_specs=[
            pl.BlockSpec((bsz, d), lambda l, n: (0, 0)),
            pl.BlockSpec((1, d, ncol), lambda l, n: (l, 0, n)),
            pl.BlockSpec((1, 1, ncol), lambda l, n: (l, 0, n)),
        ],
        out_specs=pl.BlockSpec((1, bsz, ncol), lambda l, n: (l, 0, n)),
        out_shape=jax.ShapeDtypeStruct((depth, bsz, d3), F32),
        compiler_params=pltpu.CompilerParams(dimension_semantics=("arbitrary", "arbitrary")),
        name="ada_mod",
    )(c, ada_w, ada_b.reshape(depth, 1, d3))


def _lb_kernel(logit_ref, o_ref):
    depth = logit_ref.shape[0]
    rows = [logit_ref[l:l + 1, :] for l in range(depth)]
    mx = functools.reduce(jnp.maximum, rows)
    ex = [jnp.exp(r - mx) for r in rows]
    inv = 1.0 / functools.reduce(lambda a, b: a + b, ex)
    p = [e * inv for e in ex]
    run = jnp.zeros_like(p[0])
    for l in range(depth):
        run = run + p[l]
        o_ref[l:l + 1, :] = run - p[0]


def _lb_table_call(logits):
    return pl.pallas_call(
        _lb_kernel,
        out_shape=jax.ShapeDtypeStruct(logits.shape, F32),
        name="hgrn_lb",
    )(logits)


def kernel(x, c, w_in, w_out, ada_w, ada_b, ln_g, ln_b, hgrn_lb_logits, gla_w_gk, gla_b_gk, gdn_conv_w, gdn_a_log,
           gdn_dt_bias, gain_a, gain_b, gain_c):
    depth = w_in.shape[0]
    bsz = x.shape[0]
    consts = _constants()
    col_src = _padded_column_sources()
    mix_src = _padded_mix_sources()
    gk_src = np.full((B_QKP,), -1, np.int64)
    for h in range(B_HEADS):
        gk_src[h * B_DKP:h * B_DKP + B_DK] = np.arange(h * B_DK, (h + 1) * B_DK)

    mod_all = _ada_mod_call(c, ada_w, ada_b).reshape(depth, bsz, 3, D_MODEL)
    lb_table = _lb_table_call(hgrn_lb_logits.astype(F32))

    for l in range(depth):
        w_in_p = _gather_padded(w_in[l], col_src, 1).astype(BF16)
        w_out_p = _gather_padded(w_out[l], mix_src, 0).astype(BF16)
        wgk = _gather_padded(gla_w_gk[l].astype(F32), gk_src, 1)
        wgk = jnp.zeros((LANES, B_QKP), F32).at[MISC_LR:MISC_LR + GLA_RANK].set(wgk)
        wgk_hi = wgk.astype(BF16)
        wgk_lo = (wgk - wgk_hi.astype(F32)).astype(BF16)
        bgk = _gather_padded(gla_b_gk[l].astype(F32), gk_src, 0).reshape(1, B_QKP)
        avec = jnp.zeros((1, LANES), F32).at[0, MISC_AC:MISC_AC + C_HEADS].set(gdn_a_log[l].astype(F32))
        dtvec = jnp.zeros((1, LANES), F32).at[0, MISC_AC:MISC_AC + C_HEADS].set(gdn_dt_bias[l].astype(F32))
        gain_b_p = jnp.zeros((B_DVP,), F32).at[:B_DV].set(gain_b[l].astype(F32))
        gains = jnp.concatenate([jnp.tile(gain_a[l].astype(F32), A_HEADS), jnp.tile(gain_b_p, B_HEADS),
                                 jnp.tile(gain_c[l].astype(F32), C_HEADS)]).reshape(1, D_MIXP)
        x = _hybrid_layer_call(
            x, mod_all[l], w_in_p, w_out_p, ln_g[l].reshape(1, D_MODEL), ln_b[l].reshape(1, D_MODEL),
            lb_table[l].reshape(1, A_QK), wgk_hi, wgk_lo, bgk, gdn_conv_w[l].astype(F32), avec, dtvec, gains, consts)
    return x
```

```python
import functools

import numpy as np
import jax
import jax.numpy as jnp
from jax import lax
from jax.experimental import pallas as pl
from jax.experimental.pallas import tpu as pltpu

F32 = jnp.float32
BF16 = jnp.bfloat16

D_MODEL = 1024
DEPTH = 2
CHUNK = 64
CONV_WIDTH = 4
A_HEADS, A_DK, A_DV = 4, 64, 64
B_HEADS, B_DK, B_DV = 4, 48, 96
GLA_RANK = 16
GLA_GATE_TEMP = 16.0
C_HEADS, C_DK, C_DV = 6, 64, 64
DEEPNORM_ALPHA = (2 * DEPTH) ** 0.25
LN_EPS = 1e-5
NORM_EPS = 1e-6
FORGET_FLOOR = 1e-30

A_QK = A_HEADS * A_DK
A_V = A_HEADS * A_DV
B_QK = B_HEADS * B_DK
B_V = B_HEADS * B_DV
C_QK = C_HEADS * C_DK
C_V = C_HEADS * C_DV
C_QKV = 2 * C_QK + C_V
SPLIT_SIZES = (A_QK, A_QK, A_V, A_V, B_QK, B_QK, B_V, GLA_RANK, B_V, C_QKV, C_HEADS, C_HEADS, C_V)
D_IN = sum(SPLIT_SIZES)

LANES = 128
SUBLANES = 8
VMEM_LIMIT_BYTES = 56 * 1024 * 1024

B_DKP = 64
B_DVP = 128
B_QKP = B_HEADS * B_DKP
B_VP = B_HEADS * B_DVP
SUB = 16
NSUB = CHUNK // SUB
FSUB = 32
DECAY_CAP = 60.0
CHUNKS_PER_STEP = 2

OFF_QA = 0
OFF_FA = OFF_QA + A_QK
OFF_IA = OFF_FA + A_QK
OFF_ZA = OFF_IA + A_V
OFF_QB = OFF_ZA + A_V
OFF_KB = OFF_QB + B_QKP
OFF_VB = OFF_KB + B_QKP
OFF_ZB = OFF_VB + B_VP
OFF_QKVC = OFF_ZB + B_VP
OFF_ZC = OFF_QKVC + C_QKV
OFF_MISC = OFF_ZC + C_V
D_INP = OFF_MISC + LANES
MISC_LR = 0
MISC_AC = MISC_LR + GLA_RANK
MISC_BC = MISC_AC + C_HEADS
D_MIXP = A_V + B_VP + C_V
OFF_YA, OFF_YB, OFF_YC = 0, A_V, A_V + B_VP


def _padded_column_sources():
    offs = np.concatenate([[0], np.cumsum(SPLIT_SIZES)])
    (o_qa, o_fa, o_ia, o_za, o_qb, o_kb, o_vb, o_lr, o_zb, o_qkvc, o_ac, o_bc, o_zc) = offs[:-1]
    src = np.full((D_INP,), -1, np.int64)
    for dst, s0 in ((OFF_QA, o_qa), (OFF_FA, o_fa), (OFF_IA, o_ia), (OFF_ZA, o_za)):
        src[dst:dst + A_QK] = np.arange(s0, s0 + A_QK)
    for h in range(B_HEADS):
        src[OFF_QB + h * B_DKP:OFF_QB + h * B_DKP + B_DK] = np.arange(o_qb + h * B_DK, o_qb + (h + 1) * B_DK)
        src[OFF_KB + h * B_DKP:OFF_KB + h * B_DKP + B_DK] = np.arange(o_kb + h * B_DK, o_kb + (h + 1) * B_DK)
        src[OFF_VB + h * B_DVP:OFF_VB + h * B_DVP + B_DV] = np.arange(o_vb + h * B_DV, o_vb + (h + 1) * B_DV)
        src[OFF_ZB + h * B_DVP:OFF_ZB + h * B_DVP + B_DV] = np.arange(o_zb + h * B_DV, o_zb + (h + 1) * B_DV)
    src[OFF_QKVC:OFF_QKVC + C_QKV] = np.arange(o_qkvc, o_qkvc + C_QKV)
    src[OFF_ZC:OFF_ZC + C_V] = np.arange(o_zc, o_zc + C_V)
    src[OFF_MISC + MISC_LR:OFF_MISC + MISC_LR + GLA_RANK] = np.arange(o_lr, o_lr + GLA_RANK)
    src[OFF_MISC + MISC_AC:OFF_MISC + MISC_AC + C_HEADS] = np.arange(o_ac, o_ac + C_HEADS)
    src[OFF_MISC + MISC_BC:OFF_MISC + MISC_BC + C_HEADS] = np.arange(o_bc, o_bc + C_HEADS)
    return src


def _padded_mix_sources():
    src = np.full((D_MIXP,), -1, np.int64)
    src[OFF_YA:OFF_YA + A_V] = np.arange(0, A_V)
    for h in range(B_HEADS):
        src[OFF_YB + h * B_DVP:OFF_YB + h * B_DVP + B_DV] = np.arange(A_V + h * B_DV, A_V + (h + 1) * B_DV)
    src[OFF_YC:OFF_YC + C_V] = np.arange(A_V + B_V, A_V + B_V + C_V)
    return src


def _gather_padded(w, src, axis):
    idx = jnp.asarray(np.maximum(src, 0), jnp.int32)
    valid = jnp.asarray(src >= 0)
    shape = [1] * w.ndim
    shape[axis] = src.shape[0]
    return jnp.where(valid.reshape(shape), jnp.take(w, idx, axis=axis), 0.0)


def _block_ones(rows, rgroup, cols, cgroup):
    r = np.arange(rows)[:, None] // rgroup
    c = np.arange(cols)[None, :] // cgroup
    return (r == c).astype(np.float32)


def _head_select(nheads, width, group):
    m = np.zeros((max(nheads, SUBLANES), width), np.float32)
    for h in range(nheads):
        m[h, h * group:(h + 1) * group] = 1.0
    return m


def _level_masks():
    t = np.arange(CHUNK)[:, None]
    s = np.arange(CHUNK)[None, :]
    masks = []
    m = 1
    while m < CHUNK:
        masks.append(((t // (2 * m) == s // (2 * m)) & (t % (2 * m) >= m) & (s % (2 * m) < m)).astype(np.float32))
        m *= 2
    return np.stack(masks)


def _constants():
    ltri = np.tril(np.ones((CHUNK, CHUNK), np.float32))
    e_a = np.zeros((LANES, C_QK), np.float32)
    e_b = np.zeros((LANES, C_QK), np.float32)
    for h in range(C_HEADS):
        e_a[MISC_AC + h, h * C_DK:(h + 1) * C_DK] = 1.0
        e_b[MISC_BC + h, h * C_DK:(h + 1) * C_DK] = 1.0
    return dict(
        ltri=jnp.asarray(ltri, BF16),
        e_a=jnp.asarray(e_a, BF16),
        e_b=jnp.asarray(e_b, BF16),
        ones_a=jnp.asarray(_block_ones(A_QK, A_DK, A_V, A_DV), BF16),
        ones_b=jnp.asarray(_block_ones(B_QKP, B_DKP, B_VP, B_DVP), BF16),
        ones_bv=jnp.asarray(_block_ones(B_VP, B_DVP, B_VP, B_DVP), BF16),
        ones_c=jnp.asarray(_block_ones(C_QK, C_DK, C_QK, C_DK), BF16),
        mask_a=jnp.asarray(_block_ones(A_V, A_DV, A_QK, A_DK), F32),
        mask_b=jnp.asarray(_block_ones(B_VP, B_DVP, B_QKP, B_DKP), F32),
        hmk=jnp.asarray(_head_select(A_HEADS, A_QK, A_DK), F32),
        hmv_a=jnp.asarray(_head_select(A_HEADS, A_V, A_DV), F32),
        hmv_b=jnp.asarray(_head_select(B_HEADS, B_VP, B_DVP), F32),
        lvl=jnp.asarray(_level_masks(), F32),
    )


def _sigmoid(x):
    return 1.0 / (1.0 + jnp.exp(-x))


def _softplus(x):
    return jnp.maximum(x, 0.0) + jnp.log1p(jnp.exp(-jnp.abs(x)))


def _split_bf16(a):
    hi = a.astype(BF16)
    lo = (a - hi.astype(F32)).astype(BF16)
    return hi, lo


def _dot(a, b):
    return jnp.dot(a, b, preferred_element_type=F32)


def _dot_nt(a, b):
    return lax.dot_general(a, b, (((1,), (1,)), ((), ())), preferred_element_type=F32)


def _dot_tn(a, b):
    return lax.dot_general(a, b, (((0,), (0,)), ((), ())), preferred_element_type=F32)


def _dot_lhs2(a, b_bf16):
    hi, lo = _split_bf16(a)
    return _dot(hi, b_bf16) + _dot(lo, b_bf16)


def _dot_rhs2(a_bf16, b):
    hi, lo = _split_bf16(b)
    return _dot(a_bf16, hi) + _dot(a_bf16, lo)


def _dot3(a, b):
    ah, al = _split_bf16(a)
    bh, bl = _split_bf16(b)
    return _dot(ah, bh) + (_dot(ah, bl) + _dot(al, bh))


def _dot1(a, b):
    return _dot(a.astype(BF16), b.astype(BF16))


def _layer_norm(x):
    mu = jnp.mean(x, axis=-1, keepdims=True)
    xc = x - mu
    var = jnp.mean(xc * xc, axis=-1, keepdims=True)
    return xc * lax.rsqrt(var + LN_EPS)


def _block_decay_span(bcum):
    spans = [bcum[n:n + 1] - bcum[n + FSUB - 1:n + FSUB] for n in range(0, CHUNK, FSUB)]
    return functools.reduce(jnp.maximum, spans)


def _gla_intra_bounded(q, k, v16, bcum, hmk, hmv, nheads):
    outs = []
    for i in range(CHUNK // FSUB):
        n, m = i * FSUB, (i + 1) * FSUB
        ri = bcum[n:n + 1]
        qt = q[n:m] * jnp.exp(bcum[n:m] - ri)
        ksc = (k[0:m] * jnp.exp(ri - bcum[0:m])).astype(BF16)
        qstack = jnp.concatenate([qt * hmk[h:h + 1] for h in range(nheads)], axis=0).astype(BF16)
        sst = _dot_nt(qstack, ksc)
        t_io = lax.broadcasted_iota(jnp.int32, sst.shape, 0) % FSUB + n
        s_io = lax.broadcasted_iota(jnp.int32, sst.shape, 1)
        r = _dot(jnp.where(t_io >= s_io, sst, 0.0).astype(BF16), v16[0:m])
        o_i = r[0:FSUB] * hmv[0:1]
        for h in range(1, nheads):
            o_i = o_i + r[h * FSUB:(h + 1) * FSUB] * hmv[h:h + 1]
        outs.append(o_i)
    return jnp.concatenate(outs, axis=0)


def _gla_chunk(q, k, v, bcum, st_ref, ones_kv, st_mask, hmk, hmv, nheads, bounded):
    st = st_ref[...]
    b_last = bcum[CHUNK - 1:CHUNK, :]
    o_inter = _dot_nt((q * jnp.exp(bcum)).astype(BF16), st.astype(BF16))
    v16 = v.astype(BF16)
    kdec = (k * jnp.exp(b_last - bcum)).astype(BF16)
    st_ref[...] = st * jnp.exp(b_last) + _dot_tn(v16, kdec) * st_mask
    if bounded:
        return o_inter + _gla_intra_bounded(q, k, v16, bcum, hmk, hmv, nheads)
    tio = lax.broadcasted_iota(jnp.int32, (SUB, q.shape[1]), 0)
    outs = []
    for i in range(NSUB):
        n = i * SUB
        qi = q[n:n + SUB]
        ki = k[n:n + SUB]
        bi = bcum[n:n + SUB]
        vi = v[n:n + SUB]
        xs = []
        for s in range(SUB):
            rel = jnp.minimum(bi - bi[s:s + 1], 0.0)
            x = qi * ki[s:s + 1] * jnp.exp(rel)
            xs.append(jnp.where(tio >= s, x, 0.0))
        x_all = jnp.concatenate(xs, axis=0).astype(BF16)
        r_all = _dot(x_all, ones_kv)
        o_i = r_all[0:SUB] * vi[0:1]
        for s in range(1, SUB):
            o_i = o_i + r_all[s * SUB:(s + 1) * SUB] * vi[s:s + 1]
        if i > 0:
            ri = bcum[n:n + 1]
            ksc = (k[0:n] * jnp.exp(ri - bcum[0:n])).astype(BF16)
            qt = qi * jnp.exp(bi - ri)
            qstack = jnp.concatenate([qt * hmk[h:h + 1] for h in range(nheads)], axis=0).astype(BF16)
            sst = _dot_nt(qstack, ksc)
            r_od = _dot(sst.astype(BF16), v16[0:n])
            for h in range(nheads):
                o_i = o_i + r_od[h * SUB:(h + 1) * SUB] * hmv[h:h + 1]
        outs.append(o_i)
    return jnp.concatenate(outs, axis=0) + o_inter


def _unit_lower_inverses(nmats, lvl_ref):
    r = lax.broadcasted_iota(jnp.int32, (CHUNK, CHUNK), 0)
    c = lax.broadcasted_iota(jnp.int32, (CHUNK, CHUNK), 1)
    eye = jnp.where(r == c, 1.0, 0.0)
    tinvs = [eye - n * lvl_ref[0] for n in nmats]
    for lev in range(1, lvl_ref.shape[0]):
        m = lvl_ref[lev]
        ps = [_dot1(n * m, t) for n, t in zip(nmats, tinvs)]
        qs = [_dot1(t, p) for t, p in zip(tinvs, ps)]
        tinvs = [t - q for t, q in zip(tinvs, qs)]
    return tinvs


def _layer_kernel(x_ref, mod_ref, win_ref, wout_ref, lng_ref, lnb_ref, lb_ref, wgkh_ref, wgkl_ref, bgk_ref,
                  convw_ref, avec_ref, dtvec_ref, gain_ref,
                  ltri_ref, ea_ref, eb_ref, ones_a_ref, ones_b_ref, ones_bv_ref, ones_c_ref,
                  mask_a_ref, mask_b_ref, hmk_ref, hmva_ref, hmvb_ref, lvl_ref,
                  out_ref,
                  proj_ref, y_ref, sa_ref, sb_ref, sc_ref, conv_ref):
    tile = x_ref.shape[1]

    @pl.when(pl.program_id(1) == 0)
    def _():
        sa_ref[...] = jnp.zeros_like(sa_ref)
        sb_ref[...] = jnp.zeros_like(sb_ref)
        sc_ref[...] = jnp.zeros_like(sc_ref)
        conv_ref[0:SUBLANES, :] = jnp.zeros((SUBLANES, C_QKV), F32)

    x = x_ref[0]
    shift = mod_ref[0, 0:1, :]
    scale = mod_ref[0, 1:2, :]
    gate = mod_ref[0, 2:3, :]
    h = _layer_norm(x) * (1.0 + scale) + shift
    proj_ref[...] = _dot(h.astype(BF16), win_ref[...])

    ltri = ltri_ref[...]
    hmk = hmk_ref[...]
    hmva = hmva_ref[...]
    hmvb = hmvb_ref[...]
    t_io = lax.broadcasted_iota(jnp.int32, (CHUNK, CHUNK), 0)
    s_io = lax.broadcasted_iota(jnp.int32, (CHUNK, CHUNK), 1)
    causal = t_io >= s_io
    strict = t_io > s_io

    def norm_gate(o, z, ones, dv, gain):
        ms = _dot_lhs2(o * o, ones) * (1.0 / dv)
        return (o * lax.rsqrt(ms + NORM_EPS) * gain * (z * _sigmoid(z))).astype(BF16)

    def group_body(g, carry):
        chunk_rows = [pl.ds(pl.multiple_of((g * CHUNKS_PER_STEP + i) * CHUNK, CHUNK), CHUNK)
                      for i in range(CHUNKS_PER_STEP)]

        miscs, ab = [], []
        for rows in chunk_rows:
            qa = proj_ref[rows, OFF_QA:OFF_QA + A_QK]
            fa = proj_ref[rows, OFF_FA:OFF_FA + A_QK]
            lbv = lb_ref[...]
            sg = _sigmoid(fa)
            f_a = lbv + (1.0 - lbv) * sg
            g_a = jnp.log(jnp.maximum(f_a, FORGET_FLOOR))
            k_a = (1.0 - lbv) * (1.0 - sg)
            q_a = qa * _sigmoid(qa)
            misc = proj_ref[rows, OFF_MISC:OFF_MISC + LANES]
            miscs.append(misc)
            m_hi, m_lo = _split_bf16(misc)
            gk_lin = (_dot(m_hi, wgkh_ref[...]) + (_dot(m_lo, wgkh_ref[...]) + _dot(m_hi, wgkl_ref[...]))
                      + bgk_ref[...])
            g_b = -_softplus(-gk_lin) * (1.0 / GLA_GATE_TEMP)
            ab.append((q_a, k_a, _dot_rhs2(ltri, g_a), _dot_rhs2(ltri, g_b)))

        def ab_chunks(bounded):
            for rows, (q_a, k_a, bcum_a, bcum_b) in zip(chunk_rows, ab):
                ia = proj_ref[rows, OFF_IA:OFF_IA + A_V]
                o_a = _gla_chunk(q_a, k_a, ia, bcum_a, sa_ref, ones_a_ref[...], mask_a_ref[...],
                                 hmk, hmva, A_HEADS, bounded)
                za = proj_ref[rows, OFF_ZA:OFF_ZA + A_V]
                y_ref[rows, OFF_YA:OFF_YA + A_V] = norm_gate(o_a, za, ones_a_ref[...], A_DV,
                                                             gain_ref[:, OFF_YA:OFF_YA + A_V])
                q_b = proj_ref[rows, OFF_QB:OFF_QB + B_QKP] * (B_DK ** -0.5)
                k_b = proj_ref[rows, OFF_KB:OFF_KB + B_QKP]
                v_b = proj_ref[rows, OFF_VB:OFF_VB + B_VP]
                o_b = _gla_chunk(q_b, k_b, v_b, bcum_b, sb_ref, ones_b_ref[...], mask_b_ref[...],
                                 hmk, hmvb, B_HEADS, bounded)
                zb = proj_ref[rows, OFF_ZB:OFF_ZB + B_VP]
                y_ref[rows, OFF_YB:OFF_YB + B_VP] = norm_gate(o_b, zb, ones_bv_ref[...], B_DV,
                                                              gain_ref[:, OFF_YB:OFF_YB + B_VP])

        span = functools.reduce(jnp.maximum, [_block_decay_span(b) for item in ab for b in item[2:]])
        bounded = jnp.max(span) <= DECAY_CAP
        pl.when(bounded)(lambda: ab_chunks(True))
        pl.when(jnp.logical_not(bounded))(lambda: ab_chunks(False))

        ones_c = ones_c_ref[...]
        cw = convw_ref[...]
        prep = []
        for rows, misc in zip(chunk_rows, miscs):
            raw = proj_ref[rows, OFF_QKVC:OFF_QKVC + C_QKV]
            conv_ref[SUBLANES:SUBLANES + CHUNK, :] = raw
            acc = raw * cw[CONV_WIDTH - 1:CONV_WIDTH]
            for j in range(CONV_WIDTH - 1):
                lo = SUBLANES - (CONV_WIDTH - 1) + j
                acc = acc + conv_ref[lo:lo + CHUNK, :] * cw[j:j + 1]
            conv_ref[0:SUBLANES, :] = conv_ref[CHUNK:CHUNK + SUBLANES, :]
            qkv = acc * _sigmoid(acc)
            q_c = qkv[:, 0:C_QK]
            k_c = qkv[:, C_QK:2 * C_QK]
            v_c = qkv[:, 2 * C_QK:C_QKV]
            q_c = q_c * lax.rsqrt(_dot_lhs2(q_c * q_c, ones_c) + NORM_EPS) * (C_DK ** -0.5)
            k_c = k_c * lax.rsqrt(_dot_lhs2(k_c * k_c, ones_c) + NORM_EPS)
            log_a = -jnp.exp(avec_ref[...]) * _softplus(misc + dtvec_ref[...])
            beta = _sigmoid(misc)
            bc_small = _dot_rhs2(ltri, log_a)
            bexp = _dot_lhs2(bc_small, ea_ref[...])
            beta_x = _dot_lhs2(beta, eb_ref[...])
            e_b = jnp.exp(bexp)
            bl_c = bexp[CHUNK - 1:CHUNK, :]
            kb_c = k_c * beta_x
            prep.append(dict(
                bexp=bexp, b_rows=bc_small.T, e_last=jnp.exp(bl_c),
                q16=q_c.astype(BF16), k16=k_c.astype(BF16), kb16=kb_c.astype(BF16),
                vb16=(v_c * beta_x).astype(BF16), kbe16=(kb_c * e_b).astype(BF16),
                qe16=(q_c * e_b).astype(BF16), kdl16=(k_c * jnp.exp(bl_c - bexp)).astype(BF16)))

        items = [(i, hd) for i in range(CHUNKS_PER_STEP) for hd in range(C_HEADS)]

        def hsl(hd):
            return slice(hd * C_DK, (hd + 1) * C_DK)

        kks = [_dot_nt(prep[i]["kb16"][:, hsl(hd)], prep[i]["k16"][:, hsl(hd)]) for i, hd in items]
        qks = [_dot_nt(prep[i]["q16"][:, hsl(hd)], prep[i]["k16"][:, hsl(hd)]) for i, hd in items]
        decs = [jnp.exp(jnp.minimum(
            prep[i]["bexp"][:, hsl(hd)] - prep[i]["b_rows"][MISC_AC + hd:MISC_AC + hd + 1, :], 0.0))
            for i, hd in items]
        nmats = [jnp.where(strict, kk * dec, 0.0) for kk, dec in zip(kks, decs)]
        attns = [jnp.where(causal, qk * dec, 0.0).astype(BF16) for qk, dec in zip(qks, decs)]
        t16s = [t.astype(BF16) for t in _unit_lower_inverses(nmats, lvl_ref)]
        us = [_dot(t16, prep[i]["vb16"][:, hsl(hd)]) for t16, (i, hd) in zip(t16s, items)]
        ws = [_dot(t16, prep[i]["kbe16"][:, hsl(hd)]).astype(BF16) for t16, (i, hd) in zip(t16s, items)]

        for i, rows in enumerate(chunk_rows):
            base = i * C_HEADS
            s_hs = [sc_ref[hd] for hd in range(C_HEADS)]
            s16s = [s.astype(BF16) for s in s_hs]
            wss = [_dot(ws[base + hd], s16s[hd]) for hd in range(C_HEADS)]
            o_ss = [_dot(prep[i]["qe16"][:, hsl(hd)], s16s[hd]) for hd in range(C_HEADS)]
            vn16s = [(us[base + hd] - wss[hd]).astype(BF16) for hd in range(C_HEADS)]
            upds = [_dot_tn(prep[i]["kdl16"][:, hsl(hd)], vn16s[hd]) for hd in range(C_HEADS)]
            o_hs = [o_ss[hd] + _dot(attns[base + hd], vn16s[hd]) for hd in range(C_HEADS)]
            for hd in range(C_HEADS):
                sc_ref[hd] = s_hs[hd] * prep[i]["e_last"][:, hsl(hd)] + upds[hd]
            o_c = jnp.concatenate(o_hs, axis=1)
            zc = proj_ref[rows, OFF_ZC:OFF_ZC + C_V]
            y_ref[rows, OFF_YC:OFF_YC + C_V] = norm_gate(o_c, zc, ones_c, C_DV, gain_ref[:, OFF_YC:OFF_YC + C_V])
        return carry

    lax.fori_loop(0, tile // (CHUNK * CHUNKS_PER_STEP), group_body, 0)

    out = _dot(y_ref[...], wout_ref[...])
    res = DEEPNORM_ALPHA * x + gate * out
    out_ref[0] = _layer_norm(res) * lng_ref[...] + lnb_ref[...]


def _seq_tile(seq):
    tile = CHUNK * CHUNKS_PER_STEP
    assert seq % tile == 0
    while tile * 2 <= min(seq, 256) and seq % (tile * 2) == 0:
        tile *= 2
    return tile


def _const_spec(arr):
    nd = arr.ndim
    return pl.BlockSpec(arr.shape, lambda b, j, _nd=nd: (0,) * _nd)


def _hybrid_layer_call(x, mod, w_in_p, w_out_p, ln_g, ln_b, lbv, wgk_hi, wgk_lo, bgk, conv_w, avec, dtvec, gains,
                       consts):
    bsz, seq, d = x.shape
    assert d == D_MODEL and seq % CHUNK == 0
    tile = _seq_tile(seq)
    names = ("ltri", "e_a", "e_b", "ones_a", "ones_b", "ones_bv", "ones_c", "mask_a", "mask_b",
             "hmk", "hmv_a", "hmv_b", "lvl")
    small = (w_in_p, w_out_p, ln_g, ln_b, lbv, wgk_hi, wgk_lo, bgk, conv_w, avec, dtvec, gains) + tuple(
        consts[n] for n in names)
    in_specs = [
        pl.BlockSpec((1, tile, D_MODEL), lambda b, j: (b, j, 0)),
        pl.BlockSpec((1, 3, D_MODEL), lambda b, j: (b, 0, 0)),
    ] + [_const_spec(a) for a in small]
    return pl.pallas_call(
        _layer_kernel,
        grid=(bsz, seq // tile),
        in_specs=in_specs,
        out_specs=pl.BlockSpec((1, tile, D_MODEL), lambda b, j: (b, j, 0)),
        out_shape=jax.ShapeDtypeStruct((bsz, seq, D_MODEL), F32),
        scratch_shapes=[
            pltpu.VMEM((tile, D_INP), F32),
            pltpu.VMEM((tile, D_MIXP), BF16),
            pltpu.VMEM((A_V, A_QK), F32),
            pltpu.VMEM((B_VP, B_QKP), F32),
            pltpu.VMEM((C_HEADS, C_DK, C_DV), F32),
            pltpu.VMEM((SUBLANES + CHUNK, C_QKV), F32),
        ],
        compiler_params=pltpu.CompilerParams(
            dimension_semantics=("arbitrary", "arbitrary"),
            vmem_limit_bytes=VMEM_LIMIT_BYTES),
        name="hybrid_layer",
    )(x, mod, *small)


def _mod_kernel(c_ref, w_ref, b_ref, o_ref):
    c = c_ref[...]
    c_act = c * _sigmoid(c)
    o_ref[0] = _dot3(c_act, w_ref[0]) + b_ref[0]


def _ada_mod_call(c, ada_w, ada_b):
    depth, d, d3 = ada_w.shape
    bsz = c.shape[0]
    ncol = 512
    assert d3 % ncol == 0
    return pl.pallas_call(
        _mod_kernel,
        grid=(depth, d3 // ncol),
        in_specs=[
            pl.BlockSpec((bsz, d), lambda l, n: (0, 0)),
            pl.BlockSpec((1, d, ncol), lambda l, n: (l, 0, n)),
            pl.BlockSpec((1, 1, ncol), lambda l, n: (l, 0, n)),
        ],
        out_specs=pl.BlockSpec((1, bsz, ncol), lambda l, n: (l, 0, n)),
        out_shape=jax.ShapeDtypeStruct((depth, bsz, d3), F32),
        compiler_params=pltpu.CompilerParams(dimension_semantics=("arbitrary", "arbitrary")),
        name="ada_mod",
    )(c, ada_w, ada_b.reshape(depth, 1, d3))


def _lb_kernel(logit_ref, o_ref):
    depth = logit_ref.shape[0]
    rows = [logit_ref[l:l + 1, :] for l in range(depth)]
    mx = functools.reduce(jnp.maximum, rows)
    ex = [jnp.exp(r - mx) for r in rows]
    inv = 1.0 / functools.reduce(lambda a, b: a + b, ex)
    p = [e * inv for e in ex]
    run = jnp.zeros_like(p[0])
    for l in range(depth):
        run = run + p[l]
        o_ref[l:l + 1, :] = run - p[0]


def _lb_table_call(logits):
    return pl.pallas_call(
        _lb_kernel,
        out_shape=jax.ShapeDtypeStruct(logits.shape, F32),
        name="hgrn_lb",
    )(logits)


def kernel(x, c, w_in, w_out, ada_w, ada_b, ln_g, ln_b, hgrn_lb_logits, gla_w_gk, gla_b_gk, gdn_conv_w, gdn_a_log,
           gdn_dt_bias, gain_a, gain_b, gain_c):
    depth = w_in.shape[0]
    bsz = x.shape[0]
    consts = _constants()
    col_src = _padded_column_sources()
    mix_src = _padded_mix_sources()
    gk_src = np.full((B_QKP,), -1, np.int64)
    for h in range(B_HEADS):
        gk_src[h * B_DKP:h * B_DKP + B_DK] = np.arange(h * B_DK, (h + 1) * B_DK)

    mod_all = _ada_mod_call(c, ada_w, ada_b).reshape(depth, bsz, 3, D_MODEL)
    lb_table = _lb_table_call(hgrn_lb_logits.astype(F32))

    for l in range(depth):
        w_in_p = _gather_padded(w_in[l], col_src, 1).astype(BF16)
        w_out_p = _gather_padded(w_out[l], mix_src, 0).astype(BF16)
        wgk = _gather_padded(gla_w_gk[l].astype(F32), gk_src, 1)
        wgk = jnp.zeros((LANES, B_QKP), F32).at[MISC_LR:MISC_LR + GLA_RANK].set(wgk)
        wgk_hi = wgk.astype(BF16)
        wgk_lo = (wgk - wgk_hi.astype(F32)).astype(BF16)
        bgk = _gather_padded(gla_b_gk[l].astype(F32), gk_src, 0).reshape(1, B_QKP)
        avec = jnp.zeros((1, LANES), F32).at[0, MISC_AC:MISC_AC + C_HEADS].set(gdn_a_log[l].astype(F32))
        dtvec = jnp.zeros((1, LANES), F32).at[0, MISC_AC:MISC_AC + C_HEADS].set(gdn_dt_bias[l].astype(F32))
        gain_b_p = jnp.zeros((B_DVP,), F32).at[:B_DV].set(gain_b[l].astype(F32))
        gains = jnp.concatenate([jnp.tile(gain_a[l].astype(F32), A_HEADS), jnp.tile(gain_b_p, B_HEADS),
                                 jnp.tile(gain_c[l].astype(F32), C_HEADS)]).reshape(1, D_MIXP)
        x = _hybrid_layer_call(
            x, mod_all[l], w_in_p, w_out_p, ln_g[l].reshape(1, D_MODEL), ln_b[l].reshape(1, D_MODEL),
            lb_table[l].reshape(1, A_QK), wgk_hi, wgk_lo, bgk, gdn_conv_w[l].astype(F32), avec, dtvec, gains, consts)
    return x
```

```python
import functools

import numpy as np
import jax
import jax.numpy as jnp
from jax import lax
from jax.experimental import pallas as pl
from jax.experimental.pallas import tpu as pltpu

F32 = jnp.float32
BF16 = jnp.bfloat16

D_MODEL = 1024
DEPTH = 2
CHUNK = 64
CONV_WIDTH = 4
assert CONV_WIDTH == 4
A_HEADS, A_DK, A_DV = 4, 64, 64
B_HEADS, B_DK, B_DV = 4, 48, 96
GLA_RANK = 16
GLA_GATE_TEMP = 16.0
C_HEADS, C_DK, C_DV = 6, 64, 64
DEEPNORM_ALPHA = (2 * DEPTH) ** 0.25
LN_EPS = 1e-5
NORM_EPS = 1e-6
FORGET_FLOOR = 1e-30

A_QK = A_HEADS * A_DK
A_V = A_HEADS * A_DV
B_QK = B_HEADS * B_DK
B_V = B_HEADS * B_DV
C_QK = C_HEADS * C_DK
C_V = C_HEADS * C_DV
C_QKV = 2 * C_QK + C_V
SPLIT_SIZES = (A_QK, A_QK, A_V, A_V, B_QK, B_QK, B_V, GLA_RANK, B_V, C_QKV, C_HEADS, C_HEADS, C_V)
D_IN = sum(SPLIT_SIZES)

LANES = 128
SUBLANES = 8
VMEM_LIMIT_BYTES = 56 * 1024 * 1024

B_DKP = 64
B_DVP = 128
B_QKP = B_HEADS * B_DKP
B_VP = B_HEADS * B_DVP
SUB = 16
NSUB = CHUNK // SUB
PACK = 16
FSUB = 32
DECAY_CAP = 60.0
CHUNKS_PER_STEP = 4

OFF_QA = 0
OFF_FA = OFF_QA + A_QK
OFF_IA = OFF_FA + A_QK
OFF_ZA = OFF_IA + A_V
OFF_QB = OFF_ZA + A_V
OFF_KB = OFF_QB + B_QKP
OFF_VB = OFF_KB + B_QKP
OFF_ZB = OFF_VB + B_VP
OFF_QKVC = OFF_ZB + B_VP
OFF_ZC = OFF_QKVC + C_QKV
OFF_MISC = OFF_ZC + C_V
D_INP = OFF_MISC + LANES
MISC_LR = 0
MISC_AC = MISC_LR + GLA_RANK
MISC_BC = MISC_AC + C_HEADS
D_MIXP = A_V + B_VP + C_V
OFF_YA, OFF_YB, OFF_YC = 0, A_V, A_V + B_VP


def _padded_column_sources():
    offs = np.concatenate([[0], np.cumsum(SPLIT_SIZES)])
    (o_qa, o_fa, o_ia, o_za, o_qb, o_kb, o_vb, o_lr, o_zb, o_qkvc, o_ac, o_bc, o_zc) = offs[:-1]
    src = np.full((D_INP,), -1, np.int64)
    for dst, s0 in ((OFF_QA, o_qa), (OFF_FA, o_fa), (OFF_IA, o_ia), (OFF_ZA, o_za)):
        src[dst:dst + A_QK] = np.arange(s0, s0 + A_QK)
    for h in range(B_HEADS):
        src[OFF_QB + h * B_DKP:OFF_QB + h * B_DKP + B_DK] = np.arange(o_qb + h * B_DK, o_qb + (h + 1) * B_DK)
        src[OFF_KB + h * B_DKP:OFF_KB + h * B_DKP + B_DK] = np.arange(o_kb + h * B_DK, o_kb + (h + 1) * B_DK)
        src[OFF_VB + h * B_DVP:OFF_VB + h * B_DVP + B_DV] = np.arange(o_vb + h * B_DV, o_vb + (h + 1) * B_DV)
        src[OFF_ZB + h * B_DVP:OFF_ZB + h * B_DVP + B_DV] = np.arange(o_zb + h * B_DV, o_zb + (h + 1) * B_DV)
    src[OFF_QKVC:OFF_QKVC + C_QKV] = np.arange(o_qkvc, o_qkvc + C_QKV)
    src[OFF_ZC:OFF_ZC + C_V] = np.arange(o_zc, o_zc + C_V)
    src[OFF_MISC + MISC_LR:OFF_MISC + MISC_LR + GLA_RANK] = np.arange(o_lr, o_lr + GLA_RANK)
    src[OFF_MISC + MISC_AC:OFF_MISC + MISC_AC + C_HEADS] = np.arange(o_ac, o_ac + C_HEADS)
    src[OFF_MISC + MISC_BC:OFF_MISC + MISC_BC + C_HEADS] = np.arange(o_bc, o_bc + C_HEADS)
    return src


def _padded_mix_sources():
    src = np.full((D_MIXP,), -1, np.int64)
    src[OFF_YA:OFF_YA + A_V] = np.arange(0, A_V)
    for h in range(B_HEADS):
        src[OFF_YB + h * B_DVP:OFF_YB + h * B_DVP + B_DV] = np.arange(A_V + h * B_DV, A_V + (h + 1) * B_DV)
    src[OFF_YC:OFF_YC + C_V] = np.arange(A_V + B_V, A_V + B_V + C_V)
    return src


def _gather_padded(w, src, axis):
    idx = jnp.asarray(np.maximum(src, 0), jnp.int32)
    valid = jnp.asarray(src >= 0)
    shape = [1] * w.ndim
    shape[axis] = src.shape[0]
    return jnp.where(valid.reshape(shape), jnp.take(w, idx, axis=axis), 0.0)


def _block_ones(rows, rgroup, cols, cgroup):
    r = np.arange(rows)[:, None] // rgroup
    c = np.arange(cols)[None, :] // cgroup
    return (r == c).astype(np.float32)


def _head_select(nheads, width, group):
    m = np.zeros((max(nheads, SUBLANES), width), np.float32)
    for h in range(nheads):
        m[h, h * group:(h + 1) * group] = 1.0
    return m


def _level_masks():
    t = np.arange(CHUNK)[:, None]
    s = np.arange(CHUNK)[None, :]
    masks = []
    m = 1
    while m < CHUNK:
        masks.append(((t // (2 * m) == s // (2 * m)) & (t % (2 * m) >= m) & (s % (2 * m) < m)).astype(np.float32))
        m *= 2
    return np.stack(masks)


def _packed_masks():
    lvl = _level_masks()
    nlev = int(np.log2(PACK))
    lvlp = np.stack([np.tile(lvl[lev][0:PACK, 0:PACK], (1, CHUNK // PACK)) for lev in range(nlev)])
    lane_block = np.arange(CHUNK)[None, :] // PACK
    pick = np.stack([np.broadcast_to(lane_block == b, (PACK, CHUNK)) for b in range(CHUNK // PACK)])
    bdiag = (np.arange(CHUNK)[:, None] // PACK == lane_block)
    return lvlp.astype(np.float32), pick.astype(np.float32), bdiag.astype(np.float32)


def _constants():
    lvlp, pick, bdiag = _packed_masks()
    ltri = np.tril(np.ones((CHUNK, CHUNK), np.float32))
    e_a = np.zeros((LANES, C_QK), np.float32)
    e_b = np.zeros((LANES, C_QK), np.float32)
    for h in range(C_HEADS):
        e_a[MISC_AC + h, h * C_DK:(h + 1) * C_DK] = 1.0
        e_b[MISC_BC + h, h * C_DK:(h + 1) * C_DK] = 1.0
    return dict(
        ltri=jnp.asarray(ltri, BF16),
        e_a=jnp.asarray(e_a, BF16),
        e_b=jnp.asarray(e_b, BF16),
        ones_a=jnp.asarray(_block_ones(A_QK, A_DK, A_V, A_DV), BF16),
        ones_b=jnp.asarray(_block_ones(B_QKP, B_DKP, B_VP, B_DVP), BF16),
        ones_bv=jnp.asarray(_block_ones(B_VP, B_DVP, B_VP, B_DVP), BF16),
        ones_c=jnp.asarray(_block_ones(C_QK, C_DK, C_QK, C_DK), BF16),
        mask_a=jnp.asarray(_block_ones(A_V, A_DV, A_QK, A_DK), F32),
        mask_b=jnp.asarray(_block_ones(B_VP, B_DVP, B_QKP, B_DKP), F32),
        hmk=jnp.asarray(_head_select(A_HEADS, A_QK, A_DK), F32),
        hmv_a=jnp.asarray(_head_select(A_HEADS, A_V, A_DV), F32),
        hmv_b=jnp.asarray(_head_select(B_HEADS, B_VP, B_DVP), F32),
        lvl=jnp.asarray(_level_masks(), BF16),
        lvlp=jnp.asarray(lvlp, BF16),
        pick=jnp.asarray(pick, BF16),
        bdiag=jnp.asarray(bdiag, BF16),
    )


def _sigmoid(x):
    return 1.0 / (1.0 + jnp.exp(-x))


def _softplus(x):
    return jnp.maximum(x, 0.0) + jnp.log1p(jnp.exp(-jnp.abs(x)))


def _split_bf16(a):
    hi = a.astype(BF16)
    lo = (a - hi.astype(F32)).astype(BF16)
    return hi, lo


def _dot(a, b):
    return jnp.dot(a, b, preferred_element_type=F32)


def _dot_nt(a, b):
    return lax.dot_general(a, b, (((1,), (1,)), ((), ())), preferred_element_type=F32)


def _dot_tn(a, b):
    return lax.dot_general(a, b, (((0,), (0,)), ((), ())), preferred_element_type=F32)


def _dot_lhs2(a, b_bf16):
    hi, lo = _split_bf16(a)
    return _dot(hi, b_bf16) + _dot(lo, b_bf16)


def _dot_rhs2(a_bf16, b):
    hi, lo = _split_bf16(b)
    return _dot(a_bf16, hi) + _dot(a_bf16, lo)


def _dot3(a, b):
    ah, al = _split_bf16(a)
    bh, bl = _split_bf16(b)
    return _dot(ah, bh) + (_dot(ah, bl) + _dot(al, bh))


def _layer_norm(x):
    mu = jnp.mean(x, axis=-1, keepdims=True)
    xc = x - mu
    var = jnp.mean(xc * xc, axis=-1, keepdims=True)
    return xc * lax.rsqrt(var + LN_EPS)


def _block_decay_span(bcum):
    spans = [bcum[n:n + 1] - bcum[n + FSUB - 1:n + FSUB] for n in range(0, CHUNK, FSUB)]
    return functools.reduce(jnp.maximum, spans)


def _gla_intra_bounded(items, hmk):
    pairs = [(c, i) for c in range(len(items)) for i in range(CHUNK // FSUB)]
    qstacks, kscs = [], []
    for c, i in pairs:
        q, k, _, bcum, _, nheads = items[c]
        n, m = i * FSUB, (i + 1) * FSUB
        ri = bcum[n:n + 1]
        qt = q[n:m] * jnp.exp(bcum[n:m] - ri)
        kscs.append((k[0:m] * jnp.exp(ri - bcum[0:m])).astype(BF16))
        qstacks.append(jnp.concatenate([qt * hmk[h:h + 1] for h in range(nheads)], axis=0).astype(BF16))
        yield
    ssts = [_dot_nt(qs, ks) for qs, ks in zip(qstacks, kscs)]
    yield
    masked = []
    for (c, i), sst in zip(pairs, ssts):
        t_io = lax.broadcasted_iota(jnp.int32, sst.shape, 0) % FSUB + i * FSUB
        s_io = lax.broadcasted_iota(jnp.int32, sst.shape, 1)
        masked.append(jnp.where(t_io >= s_io, sst, 0.0).astype(BF16))
    rs = [_dot(ms, items[c][2][0:(i + 1) * FSUB]) for (c, i), ms in zip(pairs, masked)]
    yield
    blocks = []
    for (c, i), r in zip(pairs, rs):
        hmv, nheads = items[c][4], items[c][5]
        o_i = r[0:FSUB] * hmv[0:1]
        for h in range(1, nheads):
            o_i = o_i + r[h * FSUB:(h + 1) * FSUB] * hmv[h:h + 1]
        blocks.append(o_i)
        yield
    per = CHUNK // FSUB
    return [jnp.concatenate(blocks[c * per:(c + 1) * per], axis=0) for c in range(len(items))]


def _gla_intra_direct(q, k, v, v16, bcum, ones_kv, hmk, hmv, nheads):
    tio = lax.broadcasted_iota(jnp.int32, (SUB, q.shape[1]), 0)
    outs = []
    for i in range(NSUB):
        n = i * SUB
        qi = q[n:n + SUB]
        ki = k[n:n + SUB]
        bi = bcum[n:n + SUB]
        vi = v[n:n + SUB]
        xs = []
        for s in range(SUB):
            rel = jnp.minimum(bi - bi[s:s + 1], 0.0)
            x = qi * ki[s:s + 1] * jnp.exp(rel)
            xs.append(jnp.where(tio >= s, x, 0.0))
        x_all = jnp.concatenate(xs, axis=0).astype(BF16)
        r_all = _dot(x_all, ones_kv)
        o_i = r_all[0:SUB] * vi[0:1]
        for s in range(1, SUB):
            o_i = o_i + r_all[s * SUB:(s + 1) * SUB] * vi[s:s + 1]
        if i > 0:
            ri = bcum[n:n + 1]
            ksc = (k[0:n] * jnp.exp(ri - bcum[0:n])).astype(BF16)
            qt = qi * jnp.exp(bi - ri)
            qstack = jnp.concatenate([qt * hmk[h:h + 1] for h in range(nheads)], axis=0).astype(BF16)
            sst = _dot_nt(qstack, ksc)
            r_od = _dot(sst.astype(BF16), v16[0:n])
            for h in range(nheads):
                o_i = o_i + r_od[h * SUB:(h + 1) * SUB] * hmv[h:h + 1]
        outs.append(o_i)
    return jnp.concatenate(outs, axis=0)


def _gla_streams(streams, hmk):
    flat = [(si, c) for si, st in enumerate(streams) for c in range(len(st["chunks"]))]
    pre = []
    for si, c in flat:
        q, k, v, bcum = streams[si]["chunks"][c]
        b_last = bcum[CHUNK - 1:CHUNK, :]
        pre.append(dict(v16=v.astype(BF16), qe16=(q * jnp.exp(bcum)).astype(BF16),
                        kdec16=(k * jnp.exp(b_last - bcum)).astype(BF16), d=jnp.exp(b_last)))
        yield
    upds = [_dot_tn(p["v16"], p["kdec16"]) for p in pre]
    yield
    intra = yield from _gla_intra_bounded(
        [streams[si]["chunks"][c][0:2] + (p["v16"], streams[si]["chunks"][c][3], streams[si]["hmv"],
                                         streams[si]["nheads"]) for (si, c), p in zip(flat, pre)], hmk)
    inters = [[] for _ in streams]
    for si, st in enumerate(streams):
        state = st["st_ref"][...]
        for c in range(len(st["chunks"])):
            j = flat.index((si, c))
            inters[si].append(_dot_nt(pre[j]["qe16"], state.astype(BF16)))
            state = state * pre[j]["d"] + upds[j] * st["st_mask"]
            yield
        st["st_ref"][...] = state
    intras = [[intra[flat.index((si, c))] for c in range(len(st["chunks"]))] for si, st in enumerate(streams)]
    return inters, intras


def _unit_lower_inverses(nmats, lvl_ref, lvlp_ref, pick_ref, bdiag_ref):
    r = lax.broadcasted_iota(jnp.int32, (CHUNK, CHUNK), 0)
    c = lax.broadcasted_iota(jnp.int32, (CHUNK, CHUNK), 1)
    eye = jnp.where(r == c, 1.0, 0.0)
    nblk = CHUNK // PACK
    bdiag = bdiag_ref[...]

    def pack(x):
        return functools.reduce(lambda a, b: a + b,
                                [x[b * PACK:(b + 1) * PACK] * pick_ref[b] for b in range(nblk)])

    def unpack(xp):
        return jnp.concatenate([xp] * nblk, axis=0) * bdiag

    negs = [(-n).astype(BF16) for n in nmats]
    fulls = [(eye - n * lvl_ref[0].astype(F32)).astype(BF16) for n in nmats]
    negps = [pack(n16) for n16 in negs]
    tps = [pack(t) for t in fulls]
    for lev in range(1, lvlp_ref.shape[0]):
        pps = [_dot(np16, t).astype(BF16) for np16, t in zip(negps, fulls)]
        yield
        gps = [_dot(tp, unpack(pp)).astype(BF16) for tp, pp in zip(tps, pps)]
        yield
        tps = [tp + gp * lvlp_ref[lev] for tp, gp in zip(tps, gps)]
        fulls = [unpack(tp) for tp in tps]
    m = PACK
    lev = lvlp_ref.shape[0]
    while m < CHUNK:
        rows = [(lo, lo + m) for lo in range(m, CHUNK, 2 * m)]
        keep = [(lo - m, lo) for lo in range(m, CHUNK, 2 * m)]

        def take(x):
            return jnp.concatenate([x[a:b] for a, b in rows], axis=0)

        mask_r = take(lvl_ref[lev])
        zeros = jnp.zeros((m, CHUNK), BF16)
        prs = [_dot(take(n16), t).astype(BF16) for n16, t in zip(negs, fulls)]
        yield
        pembs = [jnp.concatenate([piece for i in range(len(rows)) for piece in (zeros, pr[i * m:(i + 1) * m])],
                                 axis=0) for pr in prs]
        grs = [_dot(take(t), pe).astype(BF16) for t, pe in zip(fulls, pembs)]
        yield
        news = [take(t) + gr * mask_r for t, gr in zip(fulls, grs)]
        fulls = [jnp.concatenate([piece for i, (a, b) in enumerate(keep)
                                  for piece in (t[a:b], new[i * m:(i + 1) * m])], axis=0)
                 for t, new in zip(fulls, news)]
        m *= 2
        lev += 1
    return fulls


def _interleave(*gens_and_steps):
    live = list(gens_and_steps)
    while live:
        for entry in list(live):
            gen, steps = entry
            try:
                for _ in range(steps):
                    next(gen)
            except StopIteration:
                live.remove(entry)


def _layer_kernel(x_ref, mod_ref, win_ref, wout_ref, lng_ref, lnb_ref, lb_ref, wgkh_ref, wgkl_ref, bgk_ref,
                  convw_ref, avec_ref, dtvec_ref, gain_ref,
                  ltri_ref, ea_ref, eb_ref, ones_a_ref, ones_b_ref, ones_bv_ref, ones_c_ref,
                  mask_a_ref, mask_b_ref, hmk_ref, hmva_ref, hmvb_ref, lvl_ref, lvlp_ref, pick_ref, bdiag_ref,
                  out_ref,
                  proj_ref, y_ref, ointer_ref, sa_ref, sb_ref, sc_ref, conv_ref):
    tile = x_ref.shape[1]

    @pl.when(pl.program_id(1) == 0)
    def _():
        sa_ref[...] = jnp.zeros_like(sa_ref)
        sb_ref[...] = jnp.zeros_like(sb_ref)
        sc_ref[...] = jnp.zeros_like(sc_ref)
        conv_ref[...] = jnp.zeros_like(conv_ref)

    x = x_ref[0]
    shift = mod_ref[0, 0:1, :]
    scale = mod_ref[0, 1:2, :]
    gate = mod_ref[0, 2:3, :]
    h = _layer_norm(x) * (1.0 + scale) + shift
    proj_ref[...] = _dot(h.astype(BF16), win_ref[...])

    ltri = ltri_ref[...]
    hmk = hmk_ref[...]
    hmva = hmva_ref[...]
    hmvb = hmvb_ref[...]
    t_io = lax.broadcasted_iota(jnp.int32, (CHUNK, CHUNK), 0)
    s_io = lax.broadcasted_iota(jnp.int32, (CHUNK, CHUNK), 1)
    causal = t_io >= s_io
    strict = t_io > s_io

    def norm_gate(o, z, ones, dv, gain):
        ms = _dot((o * o).astype(BF16), ones) * (1.0 / dv)
        return (o * lax.rsqrt(ms + NORM_EPS) * gain * (z * _sigmoid(z))).astype(BF16)

    def group_body(g, carry):
        chunk_rows = [pl.ds(pl.multiple_of((g * CHUNKS_PER_STEP + i) * CHUNK, CHUNK), CHUNK)
                      for i in range(CHUNKS_PER_STEP)]

        def ab_inputs(rows):
            qa = proj_ref[rows, OFF_QA:OFF_QA + A_QK]
            fa = proj_ref[rows, OFF_FA:OFF_FA + A_QK]
            lbv = lb_ref[...]
            sg = _sigmoid(fa)
            f_a = lbv + (1.0 - lbv) * sg
            g_a = jnp.log(jnp.maximum(f_a, FORGET_FLOOR))
            k_a = (1.0 - lbv) * (1.0 - sg)
            q_a = qa * _sigmoid(qa)
            misc = proj_ref[rows, OFF_MISC:OFF_MISC + LANES]
            m_hi, m_lo = _split_bf16(misc)
            gk_lin = (_dot(m_hi, wgkh_ref[...]) + (_dot(m_lo, wgkh_ref[...]) + _dot(m_hi, wgkl_ref[...]))
                      + bgk_ref[...])
            g_b = -_softplus(-gk_lin) * (1.0 / GLA_GATE_TEMP)
            in_a = (q_a, k_a, proj_ref[rows, OFF_IA:OFF_IA + A_V], _dot_rhs2(ltri, g_a))
            in_b = (proj_ref[rows, OFF_QB:OFF_QB + B_QKP] * (B_DK ** -0.5), proj_ref[rows, OFF_KB:OFF_KB + B_QKP],
                    proj_ref[rows, OFF_VB:OFF_VB + B_VP], _dot_rhs2(ltri, g_b))
            return in_a, in_b, misc

        def store_y_ab(rows, o_a, o_b):
            za = proj_ref[rows, OFF_ZA:OFF_ZA + A_V]
            y_ref[rows, OFF_YA:OFF_YA + A_V] = norm_gate(o_a, za, ones_a_ref[...], A_DV,
                                                         gain_ref[:, OFF_YA:OFF_YA + A_V])
            zb = proj_ref[rows, OFF_ZB:OFF_ZB + B_VP]
            y_ref[rows, OFF_YB:OFF_YB + B_VP] = norm_gate(o_b, zb, ones_bv_ref[...], B_DV,
                                                          gain_ref[:, OFF_YB:OFF_YB + B_VP])

        ab = [ab_inputs(rows) for rows in chunk_rows]
        miscs = [item[2] for item in ab]
        stream_a = dict(chunks=[item[0] for item in ab], st_ref=sa_ref, st_mask=mask_a_ref[...], hmv=hmva,
                        nheads=A_HEADS)
        stream_b = dict(chunks=[item[1] for item in ab], st_ref=sb_ref, st_mask=mask_b_ref[...], hmv=hmvb,
                        nheads=B_HEADS)

        def ab_work():
            (inter_a, inter_b), (intra_a, intra_b) = yield from _gla_streams([stream_a, stream_b], hmk)
            for i, rows in enumerate(chunk_rows):
                ointer_ref[rows, 0:A_V] = inter_a[i]
                ointer_ref[rows, A_V:A_V + B_VP] = inter_b[i]
                store_y_ab(rows, inter_a[i] + intra_a[i], inter_b[i] + intra_b[i])
                yield

        span = functools.reduce(jnp.maximum, [_block_decay_span(item[j][3]) for item in ab for j in (0, 1)])
        unbounded = jnp.logical_not(jnp.max(span) <= DECAY_CAP)

        ones_c = ones_c_ref[...]
        cw = convw_ref[...]
        prep = []
        for rows, misc in zip(chunk_rows, miscs):
            raw = proj_ref[rows, OFF_QKVC:OFF_QKVC + C_QKV]
            win = jnp.concatenate([conv_ref[...], raw], axis=0)
            conv_ref[...] = raw[CHUNK - SUBLANES:CHUNK]
            back1 = pltpu.roll(win, 1, axis=0)
            older = pltpu.roll(win * cw[1:2] + back1 * cw[0:1], 2, axis=0)
            acc = (win * cw[3:4] + back1 * cw[2:3] + older)[SUBLANES:SUBLANES + CHUNK]
            qkv = acc * _sigmoid(acc)
            q_c = qkv[:, 0:C_QK]
            k_c = qkv[:, C_QK:2 * C_QK]
            v_c = qkv[:, 2 * C_QK:C_QKV]
            q_c = q_c * lax.rsqrt(_dot((q_c * q_c).astype(BF16), ones_c) + NORM_EPS) * (C_DK ** -0.5)
            k_c = k_c * lax.rsqrt(_dot((k_c * k_c).astype(BF16), ones_c) + NORM_EPS)
            log_a = -jnp.exp(avec_ref[...]) * _softplus(misc + dtvec_ref[...])
            beta = _sigmoid(misc)
            bc_small = _dot_rhs2(ltri, log_a)
            bexp = _dot_lhs2(bc_small, ea_ref[...])
            beta_x = _dot_lhs2(beta, eb_ref[...])
            e_b = jnp.exp(bexp)
            bl_c = bexp[CHUNK - 1:CHUNK, :]
            kb_c = k_c * beta_x
            prep.append(dict(
                bexp=bexp, b_rows=bc_small.T, e_last=jnp.exp(bl_c),
                q16=q_c.astype(BF16), k16=k_c.astype(BF16), kb16=kb_c.astype(BF16),
                vb16=(v_c * beta_x).astype(BF16), kbe16=(kb_c * e_b).astype(BF16),
                qe16=(q_c * e_b).astype(BF16), kdl16=(k_c * jnp.exp(bl_c - bexp)).astype(BF16)))

        items = [(i, hd) for i in range(CHUNKS_PER_STEP) for hd in range(C_HEADS)]

        def hsl(hd):
            return slice(hd * C_DK, (hd + 1) * C_DK)

        kks = [_dot_nt(prep[i]["kb16"][:, hsl(hd)], prep[i]["k16"][:, hsl(hd)]) for i, hd in items]
        qks = [_dot_nt(prep[i]["q16"][:, hsl(hd)], prep[i]["k16"][:, hsl(hd)]) for i, hd in items]
        decs = [jnp.exp(jnp.minimum(
            prep[i]["bexp"][:, hsl(hd)] - prep[i]["b_rows"][MISC_AC + hd:MISC_AC + hd + 1, :], 0.0))
            for i, hd in items]
        nmats = [jnp.where(strict, kk * dec, 0.0) for kk, dec in zip(kks, decs)]
        attns = [jnp.where(causal, qk * dec, 0.0).astype(BF16) for qk, dec in zip(qks, decs)]
        t16s = []

        def inverse_work():
            t16s.extend((yield from _unit_lower_inverses(nmats, lvl_ref, lvlp_ref, pick_ref, bdiag_ref)))

        _interleave((inverse_work(), 1), (ab_work(), 3))
        us = [_dot(t16, prep[i]["vb16"][:, hsl(hd)]) for t16, (i, hd) in zip(t16s, items)]
        ws = [_dot(t16, prep[i]["kbe16"][:, hsl(hd)]).astype(BF16) for t16, (i, hd) in zip(t16s, items)]
        u16s = [u.astype(BF16) for u in us]
        kws = [_dot_tn(prep[i]["kdl16"][:, hsl(hd)], w).astype(BF16) for w, (i, hd) in zip(ws, items)]
        kus = [_dot_tn(prep[i]["kdl16"][:, hsl(hd)], u16) for u16, (i, hd) in zip(u16s, items)]

        states = [sc_ref[hd] for hd in range(C_HEADS)]
        s16s = []
        for i in range(CHUNKS_PER_STEP):
            base = i * C_HEADS
            cur16 = [s.astype(BF16) for s in states]
            s16s.extend(cur16)
            moved = [_dot(kws[base + hd], cur16[hd]) for hd in range(C_HEADS)]
            states = [states[hd] * prep[i]["e_last"][:, hsl(hd)] - moved[hd] + kus[base + hd]
                      for hd in range(C_HEADS)]
        for hd in range(C_HEADS):
            sc_ref[hd] = states[hd]

        ws_os = [_dot(jnp.concatenate([w, prep[i]["qe16"][:, hsl(hd)]], axis=0), s16)
                 for w, s16, (i, hd) in zip(ws, s16s, items)]
        vn16s = [(u - wo[0:CHUNK]).astype(BF16) for u, wo in zip(us, ws_os)]
        o_items = [wo[CHUNK:2 * CHUNK] + _dot(attn, vn16) for wo, attn, vn16 in zip(ws_os, attns, vn16s)]
        for i, rows in enumerate(chunk_rows):
            o_c = jnp.concatenate(o_items[i * C_HEADS:(i + 1) * C_HEADS], axis=1)
            zc = proj_ref[rows, OFF_ZC:OFF_ZC + C_V]
            y_ref[rows, OFF_YC:OFF_YC + C_V] = norm_gate(o_c, zc, ones_c, C_DV, gain_ref[:, OFF_YC:OFF_YC + C_V])

        @pl.when(unbounded)
        def _():
            for rows in chunk_rows:
                (q_a, k_a, v_a, bcum_a), (q_b, k_b, v_b, bcum_b), _ = ab_inputs(rows)
                o_a = _gla_intra_direct(q_a, k_a, v_a, v_a.astype(BF16), bcum_a, ones_a_ref[...], hmk, hmva,
                                        A_HEADS)
                o_b = _gla_intra_direct(q_b, k_b, v_b, v_b.astype(BF16), bcum_b, ones_b_ref[...], hmk, hmvb,
                                        B_HEADS)
                store_y_ab(rows, ointer_ref[rows, 0:A_V] + o_a, ointer_ref[rows, A_V:A_V + B_VP] + o_b)
        return carry

    lax.fori_loop(0, tile // (CHUNK * CHUNKS_PER_STEP), group_body, 0)

    out = _dot(y_ref[...], wout_ref[...])
    res = DEEPNORM_ALPHA * x + gate * out
    out_ref[0] = _layer_norm(res) * lng_ref[...] + lnb_ref[...]


def _seq_tile(seq):
    tile = CHUNK * CHUNKS_PER_STEP
    assert seq % tile == 0
    while tile * 2 <= min(seq, 512) and seq % (tile * 2) == 0:
        tile *= 2
    return tile


def _const_spec(arr):
    nd = arr.ndim
    return pl.BlockSpec(arr.shape, lambda b, j, _nd=nd: (0,) * _nd)


def _hybrid_layer_call(x, mod, w_in_p, w_out_p, ln_g, ln_b, lbv, wgk_hi, wgk_lo, bgk, conv_w, avec, dtvec, gains,
                       consts):
    bsz, seq, d = x.shape
    assert d == D_MODEL and seq % CHUNK == 0
    tile = _seq_tile(seq)
    names = ("ltri", "e_a", "e_b", "ones_a", "ones_b", "ones_bv", "ones_c", "mask_a", "mask_b",
             "hmk", "hmv_a", "hmv_b", "lvl", "lvlp", "pick", "bdiag")
    small = (w_in_p, w_out_p, ln_g, ln_b, lbv, wgk_hi, wgk_lo, bgk, conv_w, avec, dtvec, gains) + tuple(
        consts[n] for n in names)
    in_specs = [
        pl.BlockSpec((1, tile, D_MODEL), lambda b, j: (b, j, 0)),
        pl.BlockSpec((1, 3, D_MODEL), lambda b, j: (b, 0, 0)),
    ] + [_const_spec(a) for a in small]
    return pl.pallas_call(
        _layer_kernel,
        grid=(bsz, seq // tile),
        in_specs=in_specs,
        out_specs=pl.BlockSpec((1, tile, D_MODEL), lambda b, j: (b, j, 0)),
        out_shape=jax.ShapeDtypeStruct((bsz, seq, D_MODEL), F32),
        scratch_shapes=[
            pltpu.VMEM((tile, D_INP), F32),
            pltpu.VMEM((tile, D_MIXP), BF16),
            pltpu.VMEM((tile, A_V + B_VP), F32),
            pltpu.VMEM((A_V, A_QK), F32),
            pltpu.VMEM((B_VP, B_QKP), F32),
            pltpu.VMEM((C_HEADS, C_DK, C_DV), F32),
            pltpu.VMEM((SUBLANES, C_QKV), F32),
        ],
        compiler_params=pltpu.CompilerParams(
            dimension_semantics=("arbitrary", "arbitrary"),
            vmem_limit_bytes=VMEM_LIMIT_BYTES),
        name="hybrid_layer",
    )(x, mod, *small)


def _mod_kernel(c_ref, w_ref, b_ref, o_ref):
    c = c_ref[...]
    c_act = c * _sigmoid(c)
    o_ref[0] = _dot3(c_act, w_ref[0]) + b_ref[0]


def _ada_mod_call(c, ada_w, ada_b):
    depth, d, d3 = ada_w.shape
    bsz = c.shape[0]
    ncol = 512
    assert d3 % ncol == 0
    return pl.pallas_call(
        _mod_kernel,
        grid=(depth, d3 // ncol),
        in_specs=[
            pl.BlockSpec((bsz, d), lambda l, n: (0, 0)),
            pl.BlockSpec((1, d, ncol), lambda l, n: (l, 0, n)),
            pl.BlockSpec((1, 1, ncol), lambda l, n: (l, 0, n)),
        ],
        out_specs=pl.BlockSpec((1, bsz, ncol), lambda l, n: (l, 0, n)),
        out_shape=jax.ShapeDtypeStruct((depth, bsz, d3), F32),
        compiler_params=pltpu.CompilerParams(dimension_semantics=("arbitrary", "arbitrary")),
        name="ada_mod",
    )(c, ada_w, ada_b.reshape(depth, 1, d3))


def _lb_kernel(logit_ref, o_ref):
    depth = logit_ref.shape[0]
    rows = [logit_ref[l:l + 1, :] for l in range(depth)]
    mx = functools.reduce(jnp.maximum, rows)
    ex = [jnp.exp(r - mx) for r in rows]
    inv = 1.0 / functools.reduce(lambda a, b: a + b, ex)
    p = [e * inv for e in ex]
    run = jnp.zeros_like(p[0])
    for l in range(depth):
        run = run + p[l]
        o_ref[l:l + 1, :] = run - p[0]


def _lb_table_call(logits):
    return pl.pallas_call(
        _lb_kernel,
        out_shape=jax.ShapeDtypeStruct(logits.shape, F32),
        name="hgrn_lb",
    )(logits)


def kernel(x, c, w_in, w_out, ada_w, ada_b, ln_g, ln_b, hgrn_lb_logits, gla_w_gk, gla_b_gk, gdn_conv_w, gdn_a_log,
           gdn_dt_bias, gain_a, gain_b, gain_c):
    depth = w_in.shape[0]
    bsz = x.shape[0]
    consts = _constants()
    col_src = _padded_column_sources()
    mix_src = _padded_mix_sources()
    gk_src = np.full((B_QKP,), -1, np.int64)
    for h in range(B_HEADS):
        gk_src[h * B_DKP:h * B_DKP + B_DK] = np.arange(h * B_DK, (h + 1) * B_DK)

    mod_all = _ada_mod_call(c, ada_w, ada_b).reshape(depth, bsz, 3, D_MODEL)
    lb_table = _lb_table_call(hgrn_lb_logits.astype(F32))

    for l in range(depth):
        w_in_p = _gather_padded(w_in[l], col_src, 1).astype(BF16)
        w_out_p = _gather_padded(w_out[l], mix_src, 0).astype(BF16)
        wgk = _gather_padded(gla_w_gk[l].astype(F32), gk_src, 1)
        wgk = jnp.zeros((LANES, B_QKP), F32).at[MISC_LR:MISC_LR + GLA_RANK].set(wgk)
        wgk_hi = wgk.astype(BF16)
        wgk_lo = (wgk - wgk_hi.astype(F32)).astype(BF16)
        bgk = _gather_padded(gla_b_gk[l].astype(F32), gk_src, 0).reshape(1, B_QKP)
        avec = jnp.zeros((1, LANES), F32).at[0, MISC_AC:MISC_AC + C_HEADS].set(gdn_a_log[l].astype(F32))
        dtvec = jnp.zeros((1, LANES), F32).at[0, MISC_AC:MISC_AC + C_HEADS].set(gdn_dt_bias[l].astype(F32))
        gain_b_p = jnp.zeros((B_DVP,), F32).at[:B_DV].set(gain_b[l].astype(F32))
        gains = jnp.concatenate([jnp.tile(gain_a[l].astype(F32), A_HEADS), jnp.tile(gain_b_p, B_HEADS),
                                 jnp.tile(gain_c[l].astype(F32), C_HEADS)]).reshape(1, D_MIXP)
        x = _hybrid_layer_call(
            x, mod_all[l], w_in_p, w_out_p, ln_g[l].reshape(1, D_MODEL), ln_b[l].reshape(1, D_MODEL),
            lb_table[l].reshape(1, A_QK), wgk_hi, wgk_lo, bgk, gdn_conv_w[l].astype(F32), avec, dtvec, gains, consts)
    return x
```

```python
import functools

import numpy as np
import jax
import jax.numpy as jnp
from jax import lax
from jax.experimental import pallas as pl
from jax.experimental.pallas import tpu as pltpu

F32 = jnp.float32
BF16 = jnp.bfloat16

D_MODEL = 1024
DEPTH = 2
CHUNK = 64
CONV_WIDTH = 4
assert CONV_WIDTH == 4
A_HEADS, A_DK, A_DV = 4, 64, 64
B_HEADS, B_DK, B_DV = 4, 48, 96
GLA_RANK = 16
GLA_GATE_TEMP = 16.0
C_HEADS, C_DK, C_DV = 6, 64, 64
DEEPNORM_ALPHA = (2 * DEPTH) ** 0.25
LN_EPS = 1e-5
NORM_EPS = 1e-6
FORGET_FLOOR = 1e-30

A_QK = A_HEADS * A_DK
A_V = A_HEADS * A_DV
B_QK = B_HEADS * B_DK
B_V = B_HEADS * B_DV
C_QK = C_HEADS * C_DK
C_V = C_HEADS * C_DV
C_QKV = 2 * C_QK + C_V
SPLIT_SIZES = (A_QK, A_QK, A_V, A_V, B_QK, B_QK, B_V, GLA_RANK, B_V, C_QKV, C_HEADS, C_HEADS, C_V)
D_IN = sum(SPLIT_SIZES)

LANES = 128
SUBLANES = 8
VMEM_LIMIT_BYTES = 56 * 1024 * 1024

B_DKP = 64
B_DVP = 128
B_QKP = B_HEADS * B_DKP
B_VP = B_HEADS * B_DVP
SUB = 16
NSUB = CHUNK // SUB
PACK = 16
FSUB = 32
DECAY_CAP = 60.0
CHUNKS_PER_STEP = 4
PROJ_COLS = 256

OFF_QA = 0
OFF_FA = OFF_QA + A_QK
OFF_IA = OFF_FA + A_QK
OFF_ZA = OFF_IA + A_V
OFF_QB = OFF_ZA + A_V
OFF_KB = OFF_QB + B_QKP
OFF_VB = OFF_KB + B_QKP
OFF_ZB = OFF_VB + B_VP
OFF_QKVC = OFF_ZB + B_VP
OFF_ZC = OFF_QKVC + C_QKV
OFF_MISC = OFF_ZC + C_V
D_INP = OFF_MISC + LANES
MISC_LR = 0
MISC_AC = MISC_LR + GLA_RANK
MISC_BC = MISC_AC + C_HEADS
D_MIXP = A_V + B_VP + C_V
OFF_YA, OFF_YB, OFF_YC = 0, A_V, A_V + B_VP


def _padded_column_sources():
    offs = np.concatenate([[0], np.cumsum(SPLIT_SIZES)])
    (o_qa, o_fa, o_ia, o_za, o_qb, o_kb, o_vb, o_lr, o_zb, o_qkvc, o_ac, o_bc, o_zc) = offs[:-1]
    src = np.full((D_INP,), -1, np.int64)
    for dst, s0 in ((OFF_QA, o_qa), (OFF_FA, o_fa), (OFF_IA, o_ia), (OFF_ZA, o_za)):
        src[dst:dst + A_QK] = np.arange(s0, s0 + A_QK)
    for h in range(B_HEADS):
        src[OFF_QB + h * B_DKP:OFF_QB + h * B_DKP + B_DK] = np.arange(o_qb + h * B_DK, o_qb + (h + 1) * B_DK)
        src[OFF_KB + h * B_DKP:OFF_KB + h * B_DKP + B_DK] = np.arange(o_kb + h * B_DK, o_kb + (h + 1) * B_DK)
        src[OFF_VB + h * B_DVP:OFF_VB + h * B_DVP + B_DV] = np.arange(o_vb + h * B_DV, o_vb + (h + 1) * B_DV)
        src[OFF_ZB + h * B_DVP:OFF_ZB + h * B_DVP + B_DV] = np.arange(o_zb + h * B_DV, o_zb + (h + 1) * B_DV)
    src[OFF_QKVC:OFF_QKVC + C_QKV] = np.arange(o_qkvc, o_qkvc + C_QKV)
    src[OFF_ZC:OFF_ZC + C_V] = np.arange(o_zc, o_zc + C_V)
    src[OFF_MISC + MISC_LR:OFF_MISC + MISC_LR + GLA_RANK] = np.arange(o_lr, o_lr + GLA_RANK)
    src[OFF_MISC + MISC_AC:OFF_MISC + MISC_AC + C_HEADS] = np.arange(o_ac, o_ac + C_HEADS)
    src[OFF_MISC + MISC_BC:OFF_MISC + MISC_BC + C_HEADS] = np.arange(o_bc, o_bc + C_HEADS)
    return src


def _padded_mix_sources():
    src = np.full((D_MIXP,), -1, np.int64)
    src[OFF_YA:OFF_YA + A_V] = np.arange(0, A_V)
    for h in range(B_HEADS):
        src[OFF_YB + h * B_DVP:OFF_YB + h * B_DVP + B_DV] = np.arange(A_V + h * B_DV, A_V + (h + 1) * B_DV)
    src[OFF_YC:OFF_YC + C_V] = np.arange(A_V + B_V, A_V + B_V + C_V)
    return src


def _gather_padded(w, src, axis):
    pieces = []
    start = 0
    for i in range(1, len(src) + 1):
        run_ends = (i == len(src) or (src[i] < 0) != (src[start] < 0)
                    or (src[start] >= 0 and src[i] != src[i - 1] + 1))
        if run_ends:
            if src[start] < 0:
                shape = list(w.shape)
                shape[axis] = i - start
                pieces.append(jnp.zeros(shape, w.dtype))
            else:
                pieces.append(lax.slice_in_dim(w, int(src[start]), int(src[i - 1]) + 1, axis=axis))
            start = i
    return jnp.concatenate(pieces, axis=axis)


def _block_ones(rows, rgroup, cols, cgroup):
    r = np.arange(rows)[:, None] // rgroup
    c = np.arange(cols)[None, :] // cgroup
    return (r == c).astype(np.float32)


def _head_select(nheads, width, group):
    m = np.zeros((max(nheads, SUBLANES), width), np.float32)
    for h in range(nheads):
        m[h, h * group:(h + 1) * group] = 1.0
    return m


def _level_masks():
    t = np.arange(CHUNK)[:, None]
    s = np.arange(CHUNK)[None, :]
    masks = []
    m = 1
    while m < CHUNK:
        masks.append(((t // (2 * m) == s // (2 * m)) & (t % (2 * m) >= m) & (s % (2 * m) < m)).astype(np.float32))
        m *= 2
    return np.stack(masks)


def _packed_masks():
    lvl = _level_masks()
    nlev = int(np.log2(PACK))
    lvlp = np.stack([np.tile(lvl[lev][0:PACK, 0:PACK], (1, CHUNK // PACK)) for lev in range(nlev)])
    lane_block = np.arange(CHUNK)[None, :] // PACK
    pick = np.stack([np.broadcast_to(lane_block == b, (PACK, CHUNK)) for b in range(CHUNK // PACK)])
    bdiag = (np.arange(CHUNK)[:, None] // PACK == lane_block)
    return lvlp.astype(np.float32), pick.astype(np.float32), bdiag.astype(np.float32)


def _constants():
    lvlp, pick, bdiag = _packed_masks()
    ltri = np.tril(np.ones((CHUNK, CHUNK), np.float32))
    e_a = np.zeros((LANES, C_QK), np.float32)
    e_b = np.zeros((LANES, C_QK), np.float32)
    for h in range(C_HEADS):
        e_a[MISC_AC + h, h * C_DK:(h + 1) * C_DK] = 1.0
        e_b[MISC_BC + h, h * C_DK:(h + 1) * C_DK] = 1.0
    return dict(
        ltri=jnp.asarray(ltri, BF16),
        e_a=jnp.asarray(e_a, BF16),
        e_b=jnp.asarray(e_b, BF16),
        ones_a=jnp.asarray(_block_ones(A_QK, A_DK, A_V, A_DV), BF16),
        ones_b=jnp.asarray(_block_ones(B_QKP, B_DKP, B_VP, B_DVP), BF16),
        ones_bv=jnp.asarray(_block_ones(B_VP, B_DVP, B_VP, B_DVP), BF16),
        ones_c=jnp.asarray(_block_ones(C_QK, C_DK, C_QK, C_DK), BF16),
        mask_a=jnp.asarray(_block_ones(A_V, A_DV, A_QK, A_DK), F32),
        mask_b=jnp.asarray(_block_ones(B_VP, B_DVP, B_QKP, B_DKP), F32),
        hmk=jnp.asarray(_head_select(A_HEADS, A_QK, A_DK), F32),
        hmv_a=jnp.asarray(_head_select(A_HEADS, A_V, A_DV), F32),
        hmv_b=jnp.asarray(_head_select(B_HEADS, B_VP, B_DVP), F32),
        lvl=jnp.asarray(_level_masks(), BF16),
        lvlp=jnp.asarray(lvlp, BF16),
        pick=jnp.asarray(pick, BF16),
        bdiag=jnp.asarray(bdiag, BF16),
    )


def _sigmoid(x):
    return 1.0 / (1.0 + jnp.exp(-x))


def _softplus(x):
    return jnp.maximum(x, 0.0) + jnp.log1p(jnp.exp(-jnp.abs(x)))


def _split_bf16(a):
    hi = a.astype(BF16)
    lo = (a - hi.astype(F32)).astype(BF16)
    return hi, lo


def _dot(a, b):
    return jnp.dot(a, b, preferred_element_type=F32)


def _dot_nt(a, b):
    return lax.dot_general(a, b, (((1,), (1,)), ((), ())), preferred_element_type=F32)


def _dot_tn(a, b):
    return lax.dot_general(a, b, (((0,), (0,)), ((), ())), preferred_element_type=F32)


def _dot_lhs2(a, b_bf16):
    hi, lo = _split_bf16(a)
    return _dot(hi, b_bf16) + _dot(lo, b_bf16)


def _dot_rhs2(a_bf16, b):
    hi, lo = _split_bf16(b)
    return _dot(a_bf16, hi) + _dot(a_bf16, lo)


def _dot3(a, b):
    ah, al = _split_bf16(a)
    bh, bl = _split_bf16(b)
    return _dot(ah, bh) + (_dot(ah, bl) + _dot(al, bh))


def _layer_norm(x):
    mu = jnp.mean(x, axis=-1, keepdims=True)
    xc = x - mu
    var = jnp.mean(xc * xc, axis=-1, keepdims=True)
    return xc * lax.rsqrt(var + LN_EPS)


def _block_decay_span(bcum):
    spans = [bcum[n:n + 1] - bcum[n + FSUB - 1:n + FSUB] for n in range(0, CHUNK, FSUB)]
    return functools.reduce(jnp.maximum, spans)


def _gla_intra_bounded(items, hmk):
    pairs = [(c, i) for c in range(len(items)) for i in range(CHUNK // FSUB)]
    qstacks, kscs = [], []
    for c, i in pairs:
        q, k, _, bcum, _, nheads = items[c]
        n, m = i * FSUB, (i + 1) * FSUB
        ri = bcum[n:n + 1]
        qt = q[n:m] * jnp.exp(bcum[n:m] - ri)
        kscs.append((k[0:m] * jnp.exp(ri - bcum[0:m])).astype(BF16))
        qstacks.append(jnp.concatenate([qt * hmk[h:h + 1] for h in range(nheads)], axis=0).astype(BF16))
        yield
    ssts = [_dot_nt(qs, ks) for qs, ks in zip(qstacks, kscs)]
    yield
    masked = []
    for (c, i), sst in zip(pairs, ssts):
        t_io = lax.broadcasted_iota(jnp.int32, sst.shape, 0) % FSUB + i * FSUB
        s_io = lax.broadcasted_iota(jnp.int32, sst.shape, 1)
        masked.append(jnp.where(t_io >= s_io, sst, 0.0).astype(BF16))
    rs = [_dot(ms, items[c][2][0:(i + 1) * FSUB]) for (c, i), ms in zip(pairs, masked)]
    yield
    blocks = []
    for (c, i), r in zip(pairs, rs):
        hmv, nheads = items[c][4], items[c][5]
        o_i = r[0:FSUB] * hmv[0:1]
        for h in range(1, nheads):
            o_i = o_i + r[h * FSUB:(h + 1) * FSUB] * hmv[h:h + 1]
        blocks.append(o_i)
        yield
    per = CHUNK // FSUB
    return [jnp.concatenate(blocks[c * per:(c + 1) * per], axis=0) for c in range(len(items))]


def _gla_intra_direct(q, k, v, v16, bcum, ones_kv, hmk, hmv, nheads):
    tio = lax.broadcasted_iota(jnp.int32, (SUB, q.shape[1]), 0)
    outs = []
    for i in range(NSUB):
        n = i * SUB
        qi = q[n:n + SUB]
        ki = k[n:n + SUB]
        bi = bcum[n:n + SUB]
        vi = v[n:n + SUB]
        xs = []
        for s in range(SUB):
            rel = jnp.minimum(bi - bi[s:s + 1], 0.0)
            x = qi * ki[s:s + 1] * jnp.exp(rel)
            xs.append(jnp.where(tio >= s, x, 0.0))
        x_all = jnp.concatenate(xs, axis=0).astype(BF16)
        r_all = _dot(x_all, ones_kv)
        o_i = r_all[0:SUB] * vi[0:1]
        for s in range(1, SUB):
            o_i = o_i + r_all[s * SUB:(s + 1) * SUB] * vi[s:s + 1]
        if i > 0:
            ri = bcum[n:n + 1]
            ksc = (k[0:n] * jnp.exp(ri - bcum[0:n])).astype(BF16)
            qt = qi * jnp.exp(bi - ri)
            qstack = jnp.concatenate([qt * hmk[h:h + 1] for h in range(nheads)], axis=0).astype(BF16)
            sst = _dot_nt(qstack, ksc)
            r_od = _dot(sst.astype(BF16), v16[0:n])
            for h in range(nheads):
                o_i = o_i + r_od[h * SUB:(h + 1) * SUB] * hmv[h:h + 1]
        outs.append(o_i)
    return jnp.concatenate(outs, axis=0)


def _gla_streams(streams, hmk):
    flat = [(si, c) for si, st in enumerate(streams) for c in range(len(st["chunks"]))]
    pre = []
    for si, c in flat:
        q, k, v, bcum = streams[si]["chunks"][c]
        b_last = bcum[CHUNK - 1:CHUNK, :]
        pre.append(dict(v16=v.astype(BF16), qe16=(q * jnp.exp(bcum)).astype(BF16),
                        kdec16=(k * jnp.exp(b_last - bcum)).astype(BF16), d=jnp.exp(b_last)))
        yield
    upds = [_dot_tn(p["v16"], p["kdec16"]) for p in pre]
    yield
    intra = yield from _gla_intra_bounded(
        [streams[si]["chunks"][c][0:2] + (p["v16"], streams[si]["chunks"][c][3], streams[si]["hmv"],
                                         streams[si]["nheads"]) for (si, c), p in zip(flat, pre)], hmk)
    inters = [[] for _ in streams]
    for si, st in enumerate(streams):
        state = st["st_ref"][...]
        for c in range(len(st["chunks"])):
            j = flat.index((si, c))
            inters[si].append(_dot_nt(pre[j]["qe16"], state.astype(BF16)))
            state = state * pre[j]["d"] + upds[j] * st["st_mask"]
            yield
        st["st_ref"][...] = state
    intras = [[intra[flat.index((si, c))] for c in range(len(st["chunks"]))] for si, st in enumerate(streams)]
    return inters, intras


def _unit_lower_inverses(nmats, lvl_ref, lvlp_ref, pick_ref, bdiag_ref):
    r = lax.broadcasted_iota(jnp.int32, (CHUNK, CHUNK), 0)
    c = lax.broadcasted_iota(jnp.int32, (CHUNK, CHUNK), 1)
    eye = jnp.where(r == c, 1.0, 0.0)
    nblk = CHUNK // PACK
    bdiag = bdiag_ref[...]

    def pack(x):
        return functools.reduce(lambda a, b: a + b,
                                [x[b * PACK:(b + 1) * PACK] * pick_ref[b] for b in range(nblk)])

    def unpack(xp):
        return jnp.concatenate([xp] * nblk, axis=0) * bdiag

    negs = [(-n).astype(BF16) for n in nmats]
    fulls = [(eye - n * lvl_ref[0].astype(F32)).astype(BF16) for n in nmats]
    negps = [pack(n16) for n16 in negs]
    tps = [pack(t) for t in fulls]
    for lev in range(1, lvlp_ref.shape[0]):
        pps = [_dot(np16, t).astype(BF16) for np16, t in zip(negps, fulls)]
        yield
        gps = [_dot(tp, unpack(pp)).astype(BF16) for tp, pp in zip(tps, pps)]
        yield
        tps = [tp + gp * lvlp_ref[lev] for tp, gp in zip(tps, gps)]
        fulls = [unpack(tp) for tp in tps]
    m = PACK
    lev = lvlp_ref.shape[0]
    while m < CHUNK:
        rows = [(lo, lo + m) for lo in range(m, CHUNK, 2 * m)]
        keep = [(lo - m, lo) for lo in range(m, CHUNK, 2 * m)]

        def take(x):
            return jnp.concatenate([x[a:b] for a, b in rows], axis=0)

        mask_r = take(lvl_ref[lev])
        zeros = jnp.zeros((m, CHUNK), BF16)
        prs = [_dot(take(n16), t).astype(BF16) for n16, t in zip(negs, fulls)]
        yield
        pembs = [jnp.concatenate([piece for i in range(len(rows)) for piece in (zeros, pr[i * m:(i + 1) * m])],
                                 axis=0) for pr in prs]
        grs = [_dot(take(t), pe).astype(BF16) for t, pe in zip(fulls, pembs)]
        yield
        news = [take(t) + gr * mask_r for t, gr in zip(fulls, grs)]
        fulls = [jnp.concatenate([piece for i, (a, b) in enumerate(keep)
                                  for piece in (t[a:b], new[i * m:(i + 1) * m])], axis=0)
                 for t, new in zip(fulls, news)]
        m *= 2
        lev += 1
    return fulls


def _interleave(*gens_and_steps):
    live = list(gens_and_steps)
    while live:
        for entry in list(live):
            gen, steps = entry
            try:
                for _ in range(steps):
                    next(gen)
            except StopIteration:
                live.remove(entry)


def _layer_kernel(x_ref, mod_ref, win_ref, wout_ref, lng_ref, lnb_ref, lb_ref, wgkh_ref, wgkl_ref, bgk_ref,
                  convw_ref, avec_ref, dtvec_ref, gain_ref,
                  ltri_ref, ea_ref, eb_ref, ones_a_ref, ones_b_ref, ones_bv_ref, ones_c_ref,
                  mask_a_ref, mask_b_ref, hmk_ref, hmva_ref, hmvb_ref, lvl_ref, lvlp_ref, pick_ref, bdiag_ref,
                  out_ref,
                  h16_ref, proj_ref, y_ref, oproj_ref, ointer_ref, sa_ref, sb_ref, sc_ref, conv_ref):
    tile = x_ref.shape[1]

    @pl.when(pl.program_id(1) == 0)
    def _():
        sa_ref[...] = jnp.zeros_like(sa_ref)
        sb_ref[...] = jnp.zeros_like(sb_ref)
        sc_ref[...] = jnp.zeros_like(sc_ref)
        conv_ref[...] = jnp.zeros_like(conv_ref)

    shift = mod_ref[0, 0:1, :]
    scale = mod_ref[0, 1:2, :]
    gate = mod_ref[0, 2:3, :]
    group_len = CHUNK * CHUNKS_PER_STEP
    ngroups = tile // group_len
    h16_ref[...] = (_layer_norm(x_ref[0]) * (1.0 + scale) + shift).astype(BF16)

    def project_rows(g):
        r0 = g * group_len
        for c0 in range(0, D_INP, PROJ_COLS):
            c1 = min(c0 + PROJ_COLS, D_INP)
            proj_ref[r0:r0 + group_len, c0:c1] = _dot(h16_ref[r0:r0 + group_len, :], win_ref[:, c0:c1])
            yield

    def finish_rows(g):
        r0 = g * group_len
        for c0 in range(0, D_MODEL, PROJ_COLS):
            oproj_ref[r0:r0 + group_len, c0:c0 + PROJ_COLS] = _dot(y_ref[r0:r0 + group_len, :],
                                                                   wout_ref[:, c0:c0 + PROJ_COLS])
            yield
        for i in range(CHUNKS_PER_STEP):
            a, b = r0 + i * CHUNK, r0 + (i + 1) * CHUNK
            res = DEEPNORM_ALPHA * x_ref[0, a:b, :] + gate * oproj_ref[a:b, :]
            out_ref[0, a:b, :] = _layer_norm(res) * lng_ref[...] + lnb_ref[...]
            yield

    def chain(*gens):
        for gen in gens:
            yield from gen

    for _ in project_rows(0):
        pass

    ltri = ltri_ref[...]
    hmk = hmk_ref[...]
    hmva = hmva_ref[...]
    hmvb = hmvb_ref[...]
    t_io = lax.broadcasted_iota(jnp.int32, (CHUNK, CHUNK), 0)
    s_io = lax.broadcasted_iota(jnp.int32, (CHUNK, CHUNK), 1)
    causal = t_io >= s_io
    strict = t_io > s_io

    def norm_gate(o, z, ones, dv, gain):
        ms = _dot((o * o).astype(BF16), ones) * (1.0 / dv)
        return (o * lax.rsqrt(ms + NORM_EPS) * gain * (z * _sigmoid(z))).astype(BF16)

    fallbacks = []

    def group_body(g, filler):
        chunk_rows = [pl.ds((g * CHUNKS_PER_STEP + i) * CHUNK, CHUNK) for i in range(CHUNKS_PER_STEP)]

        def fill(steps):
            for _ in range(steps):
                next(filler, None)

        def ab_inputs(rows):
            qa = proj_ref[rows, OFF_QA:OFF_QA + A_QK]
            fa = proj_ref[rows, OFF_FA:OFF_FA + A_QK]
            lbv = lb_ref[...]
            sg = _sigmoid(fa)
            f_a = lbv + (1.0 - lbv) * sg
            g_a = jnp.log(jnp.maximum(f_a, FORGET_FLOOR))
            k_a = (1.0 - lbv) * (1.0 - sg)
            q_a = qa * _sigmoid(qa)
            misc = proj_ref[rows, OFF_MISC:OFF_MISC + LANES]
            m_hi, m_lo = _split_bf16(misc)
            gk_lin = (_dot(m_hi, wgkh_ref[...]) + (_dot(m_lo, wgkh_ref[...]) + _dot(m_hi, wgkl_ref[...]))
                      + bgk_ref[...])
            g_b = -_softplus(-gk_lin) * (1.0 / GLA_GATE_TEMP)
            in_a = (q_a, k_a, proj_ref[rows, OFF_IA:OFF_IA + A_V], _dot_rhs2(ltri, g_a))
            in_b = (proj_ref[rows, OFF_QB:OFF_QB + B_QKP] * (B_DK ** -0.5), proj_ref[rows, OFF_KB:OFF_KB + B_QKP],
                    proj_ref[rows, OFF_VB:OFF_VB + B_VP], _dot_rhs2(ltri, g_b))
            return in_a, in_b, misc

        def store_y_ab(rows, o_a, o_b):
            za = proj_ref[rows, OFF_ZA:OFF_ZA + A_V]
            y_ref[rows, OFF_YA:OFF_YA + A_V] = norm_gate(o_a, za, ones_a_ref[...], A_DV,
                                                         gain_ref[:, OFF_YA:OFF_YA + A_V])
            zb = proj_ref[rows, OFF_ZB:OFF_ZB + B_VP]
            y_ref[rows, OFF_YB:OFF_YB + B_VP] = norm_gate(o_b, zb, ones_bv_ref[...], B_DV,
                                                          gain_ref[:, OFF_YB:OFF_YB + B_VP])

        ab = []
        for rows in chunk_rows:
            ab.append(ab_inputs(rows))
            fill(1)
        miscs = [item[2] for item in ab]
        stream_a = dict(chunks=[item[0] for item in ab], st_ref=sa_ref, st_mask=mask_a_ref[...], hmv=hmva,
                        nheads=A_HEADS)
        stream_b = dict(chunks=[item[1] for item in ab], st_ref=sb_ref, st_mask=mask_b_ref[...], hmv=hmvb,
                        nheads=B_HEADS)

        def ab_work():
            (inter_a, inter_b), (intra_a, intra_b) = yield from _gla_streams([stream_a, stream_b], hmk)
            for i, rows in enumerate(chunk_rows):
                ointer_ref[rows, 0:A_V] = inter_a[i]
                ointer_ref[rows, A_V:A_V + B_VP] = inter_b[i]
                store_y_ab(rows, inter_a[i] + intra_a[i], inter_b[i] + intra_b[i])
                yield

        span = functools.reduce(jnp.maximum, [_block_decay_span(item[j][3]) for item in ab for j in (0, 1)])
        unbounded = jnp.logical_not(jnp.max(span) <= DECAY_CAP)

        ones_c = ones_c_ref[...]
        cw = convw_ref[...]
        prep = []
        for rows, misc in zip(chunk_rows, miscs):
            raw = proj_ref[rows, OFF_QKVC:OFF_QKVC + C_QKV]
            win = jnp.concatenate([conv_ref[...], raw], axis=0)
            conv_ref[...] = raw[CHUNK - SUBLANES:CHUNK]
            back1 = pltpu.roll(win, 1, axis=0)
            older = pltpu.roll(win * cw[1:2] + back1 * cw[0:1], 2, axis=0)
            acc = (win * cw[3:4] + back1 * cw[2:3] + older)[SUBLANES:SUBLANES + CHUNK]
            qkv = acc * _sigmoid(acc)
            q_c = qkv[:, 0:C_QK]
            k_c = qkv[:, C_QK:2 * C_QK]
            v_c = qkv[:, 2 * C_QK:C_QKV]
            q_c = q_c * lax.rsqrt(_dot((q_c * q_c).astype(BF16), ones_c) + NORM_EPS) * (C_DK ** -0.5)
            k_c = k_c * lax.rsqrt(_dot((k_c * k_c).astype(BF16), ones_c) + NORM_EPS)
            log_a = -jnp.exp(avec_ref[...]) * _softplus(misc + dtvec_ref[...])
            beta = _sigmoid(misc)
            bc_small = _dot_rhs2(ltri, log_a)
            bexp = _dot_lhs2(bc_small, ea_ref[...])
            beta_x = _dot_lhs2(beta, eb_ref[...])
            e_b = jnp.exp(bexp)
            bl_c = bexp[CHUNK - 1:CHUNK, :]
            kb_c = k_c * beta_x
            prep.append(dict(
                bexp=bexp, b_rows=bc_small.T, e_last=jnp.exp(bl_c),
                q16=q_c.astype(BF16), k16=k_c.astype(BF16), kb16=kb_c.astype(BF16),
                vb16=(v_c * beta_x).astype(BF16), kbe16=(kb_c * e_b).astype(BF16),
                qe16=(q_c * e_b).astype(BF16), kdl16=(k_c * jnp.exp(bl_c - bexp)).astype(BF16)))
            fill(2)

        items =[(i, hd) for i in range(CHUNKS_PER_STEP) for hd in range(C_HEADS)]

        def hsl(hd):
            return slice(hd * C_DK, (hd + 1) * C_DK)

        kks = [_dot_nt(prep[i]["kb16"][:, hsl(hd)], prep[i]["k16"][:, hsl(hd)]) for i, hd in items]
        qks = [_dot_nt(prep[i]["q16"][:, hsl(hd)], prep[i]["k16"][:, hsl(hd)]) for i, hd in items]
        decs = [jnp.exp(jnp.minimum(
            prep[i]["bexp"][:, hsl(hd)] - prep[i]["b_rows"][MISC_AC + hd:MISC_AC + hd + 1, :], 0.0))
            for i, hd in items]
        nmats = [jnp.where(strict, kk * dec, 0.0) for kk, dec in zip(kks, decs)]
        attns = [jnp.where(causal, qk * dec, 0.0).astype(BF16) for qk, dec in zip(qks, decs)]
        t16s = []

        def inverse_work():
            t16s.extend((yield from _unit_lower_inverses(nmats, lvl_ref, lvlp_ref, pick_ref, bdiag_ref)))

        _interleave((inverse_work(), 1), (ab_work(), 3), (filler, 1))
        us = [_dot(t16, prep[i]["vb16"][:, hsl(hd)]) for t16, (i, hd) in zip(t16s, items)]
        ws = [_dot(t16, prep[i]["kbe16"][:, hsl(hd)]).astype(BF16) for t16, (i, hd) in zip(t16s, items)]
        u16s = [u.astype(BF16) for u in us]
        kws = [_dot_tn(prep[i]["kdl16"][:, hsl(hd)], w).astype(BF16) for w, (i, hd) in zip(ws, items)]
        kus = [_dot_tn(prep[i]["kdl16"][:, hsl(hd)], u16) for u16, (i, hd) in zip(u16s, items)]

        states = [sc_ref[hd] for hd in range(C_HEADS)]
        s16s = []
        for i in range(CHUNKS_PER_STEP):
            base = i * C_HEADS
            cur16 = [s.astype(BF16) for s in states]
            s16s.extend(cur16)
            moved = [_dot(kws[base + hd], cur16[hd]) for hd in range(C_HEADS)]
            states = [states[hd] * prep[i]["e_last"][:, hsl(hd)] - moved[hd] + kus[base + hd]
                      for hd in range(C_HEADS)]
        for hd in range(C_HEADS):
            sc_ref[hd] = states[hd]

        ws_os = [_dot(jnp.concatenate([w, prep[i]["qe16"][:, hsl(hd)]], axis=0), s16)
                 for w, s16, (i, hd) in zip(ws, s16s, items)]
        vn16s = [(u - wo[0:CHUNK]).astype(BF16) for u, wo in zip(us, ws_os)]
        o_items = [wo[CHUNK:2 * CHUNK] + _dot(attn, vn16) for wo, attn, vn16 in zip(ws_os, attns, vn16s)]
        for i, rows in enumerate(chunk_rows):
            o_c = jnp.concatenate(o_items[i * C_HEADS:(i + 1) * C_HEADS], axis=1)
            zc = proj_ref[rows, OFF_ZC:OFF_ZC + C_V]
            y_ref[rows, OFF_YC:OFF_YC + C_V] = norm_gate(o_c, zc, ones_c, C_DV, gain_ref[:, OFF_YC:OFF_YC + C_V])

        def redo_direct():
            for rows in chunk_rows:
                (q_a, k_a, v_a, bcum_a), (q_b, k_b, v_b, bcum_b), _ = ab_inputs(rows)
                o_a = _gla_intra_direct(q_a, k_a, v_a, v_a.astype(BF16), bcum_a, ones_a_ref[...], hmk, hmva,
                                        A_HEADS)
                o_b = _gla_intra_direct(q_b, k_b, v_b, v_b.astype(BF16), bcum_b, ones_b_ref[...], hmk, hmvb,
                                        B_HEADS)
                store_y_ab(rows, ointer_ref[rows, 0:A_V] + o_a, ointer_ref[rows, A_V:A_V + B_VP] + o_b)
            for _ in finish_rows(g):
                pass

        fallbacks.append((unbounded, redo_direct))
        for _ in filler:
            pass

    for g in range(ngroups):
        fillers = []
        if g > 0:
            fillers.append(finish_rows(g - 1))
        if g + 1 < ngroups:
            fillers.append(project_rows(g + 1))
        group_body(g, chain(*fillers))
    for _ in finish_rows(ngroups - 1):
        pass
    for unbounded, redo_direct in fallbacks:
        pl.when(unbounded)(redo_direct)


def _seq_tile(seq):
    tile = CHUNK * CHUNKS_PER_STEP
    assert seq % tile == 0
    while tile * 2 <= min(seq, 512) and seq % (tile * 2) == 0:
        tile *= 2
    return tile


def _const_spec(arr):
    nd = arr.ndim
    return pl.BlockSpec(arr.shape, lambda b, j, _nd=nd: (0,) * _nd)


def _hybrid_layer_call(x, mod, w_in_p, w_out_p, ln_g, ln_b, lbv, wgk_hi, wgk_lo, bgk, conv_w, avec, dtvec, gains,
                       consts):
    bsz, seq, d = x.shape
    assert d == D_MODEL and seq % CHUNK == 0
    tile = _seq_tile(seq)
    names = ("ltri", "e_a", "e_b", "ones_a", "ones_b", "ones_bv", "ones_c", "mask_a", "mask_b",
             "hmk", "hmv_a", "hmv_b", "lvl", "lvlp", "pick", "bdiag")
    small = (w_in_p, w_out_p, ln_g, ln_b, lbv, wgk_hi, wgk_lo, bgk, conv_w, avec, dtvec, gains) + tuple(
        consts[n] for n in names)
    in_specs = [
        pl.BlockSpec((1, tile, D_MODEL), lambda b, j: (b, j, 0)),
        pl.BlockSpec((1, 3, D_MODEL), lambda b, j: (b, 0, 0)),
    ] + [_const_spec(a) for a in small]
    return pl.pallas_call(
        _layer_kernel,
        grid=(bsz, seq // tile),
        in_specs=in_specs,
        out_specs=pl.BlockSpec((1, tile, D_MODEL), lambda b, j: (b, j, 0)),
        out_shape=jax.ShapeDtypeStruct((bsz, seq, D_MODEL), F32),
        scratch_shapes=[
            pltpu.VMEM((tile, D_MODEL), BF16),
            pltpu.VMEM((tile, D_INP), F32),
            pltpu.VMEM((tile, D_MIXP), BF16),
            pltpu.VMEM((tile, D_MODEL), F32),
            pltpu.VMEM((tile, A_V + B_VP), F32),
            pltpu.VMEM((A_V, A_QK), F32),
            pltpu.VMEM((B_VP, B_QKP), F32),
            pltpu.VMEM((C_HEADS, C_DK, C_DV), F32),
            pltpu.VMEM((SUBLANES, C_QKV), F32),
        ],
        compiler_params=pltpu.CompilerParams(
            dimension_semantics=("arbitrary", "arbitrary"),
            vmem_limit_bytes=VMEM_LIMIT_BYTES),
        name="hybrid_layer",
    )(x, mod, *small)


def _mod_kernel(c_ref, w_ref, b_ref, o_ref):
    c = c_ref[...]
    c_act = c * _sigmoid(c)
    o_ref[0] = _dot3(c_act, w_ref[0]) + b_ref[0]


def _ada_mod_call(c, ada_w, ada_b):
    depth, d, d3 = ada_w.shape
    bsz = c.shape[0]
    ncol = 512
    assert d3 % ncol == 0
    return pl.pallas_call(
        _mod_kernel,
        grid=(depth, d3 // ncol),
        in_specs=[
            pl.BlockSpec((bsz, d), lambda l, n: (0, 0)),
            pl.BlockSpec((1, d, ncol), lambda l, n: (l, 0, n)),
            pl.BlockSpec((1, 1, ncol), lambda l, n: (l, 0, n)),
        ],
        out_specs=pl.BlockSpec((1, bsz, ncol), lambda l, n: (l, 0, n)),
        out_shape=jax.ShapeDtypeStruct((depth, bsz, d3), F32),
        compiler_params=pltpu.CompilerParams(dimension_semantics=("arbitrary", "arbitrary")),
        name="ada_mod",
    )(c, ada_w, ada_b.reshape(depth, 1, d3))


def _lb_kernel(logit_ref, o_ref):
    depth = logit_ref.shape[0]
    rows = [logit_ref[l:l + 1, :] for l in range(depth)]
    mx = functools.reduce(jnp.maximum, rows)
    ex = [jnp.exp(r - mx) for r in rows]
    inv = 1.0 / functools.reduce(lambda a, b: a + b, ex)
    p = [e * inv for e in ex]
    run = jnp.zeros_like(p[0])
    for l in range(depth):
        run = run + p[l]
        o_ref[l:l + 1, :] = run - p[0]


def _lb_table_call(logits):
    return pl.pallas_call(
        _lb_kernel,
        out_shape=jax.ShapeDtypeStruct(logits.shape, F32),
        name="hgrn_lb",
    )(logits)


def kernel(x, c, w_in, w_out, ada_w, ada_b, ln_g, ln_b, hgrn_lb_logits, gla_w_gk, gla_b_gk, gdn_conv_w, gdn_a_log,
           gdn_dt_bias, gain_a, gain_b, gain_c):
    depth = w_in.shape[0]
    bsz = x.shape[0]
    consts = _constants()
    col_src = _padded_column_sources()
    mix_src = _padded_mix_sources()
    gk_src = np.full((B_QKP,), -1, np.int64)
    for h in range(B_HEADS):
        gk_src[h * B_DKP:h * B_DKP + B_DK] = np.arange(h * B_DK, (h + 1) * B_DK)

    mod_all = _ada_mod_call(c, ada_w, ada_b).reshape(depth, bsz, 3, D_MODEL)
    lb_table = _lb_table_call(hgrn_lb_logits.astype(F32))

    for l in range(depth):
        w_in_p = _gather_padded(w_in[l], col_src, 1).astype(BF16)
        w_out_p = _gather_padded(w_out[l], mix_src, 0).astype(BF16)
        wgk = _gather_padded(gla_w_gk[l].astype(F32), gk_src, 1)
        wgk = jnp.zeros((LANES, B_QKP), F32).at[MISC_LR:MISC_LR + GLA_RANK].set(wgk)
        wgk_hi = wgk.astype(BF16)
        wgk_lo = (wgk - wgk_hi.astype(F32)).astype(BF16)
        bgk = _gather_padded(gla_b_gk[l].astype(F32), gk_src, 0).reshape(1, B_QKP)
        avec = jnp.zeros((1, LANES), F32).at[0, MISC_AC:MISC_AC + C_HEADS].set(gdn_a_log[l].astype(F32))
        dtvec = jnp.zeros((1, LANES), F32).at[0, MISC_AC:MISC_AC + C_HEADS].set(gdn_dt_bias[l].astype(F32))
        gain_b_p = jnp.zeros((B_DVP,), F32).at[:B_DV].set(gain_b[l].astype(F32))
        gains = jnp.concatenate([jnp.tile(gain_a[l].astype(F32), A_HEADS), jnp.tile(gain_b_p, B_HEADS),
                                 jnp.tile(gain_c[l].astype(F32), C_HEADS)]).reshape(1, D_MIXP)
        x = _hybrid_layer_call(
            x, mod_all[l], w_in_p, w_out_p, ln_g[l].reshape(1, D_MODEL), ln_b[l].reshape(1, D_MODEL),
            lb_table[l].reshape(1, A_QK), wgk_hi, wgk_lo, bgk, gdn_conv_w[l].astype(F32), avec, dtvec, gains, consts)
    return x
```

```python
import functools

import numpy as np
import jax
import jax.numpy as jnp
from jax import lax
from jax.experimental import pallas as pl
from jax.experimental.pallas import tpu as pltpu

F32 = jnp.float32
BF16 = jnp.bfloat16

D_MODEL = 1024
DEPTH = 2
CHUNK = 64
CONV_WIDTH = 4
assert CONV_WIDTH == 4
A_HEADS, A_DK, A_DV = 4, 64, 64
B_HEADS, B_DK, B_DV = 4, 48, 96
GLA_RANK = 16
GLA_GATE_TEMP = 16.0
C_HEADS, C_DK, C_DV = 6, 64, 64
DEEPNORM_ALPHA = (2 * DEPTH) ** 0.25
LN_EPS = 1e-5
NORM_EPS = 1e-6
FORGET_FLOOR = 1e-30

A_QK = A_HEADS * A_DK
A_V = A_HEADS * A_DV
B_QK = B_HEADS * B_DK
B_V = B_HEADS * B_DV
C_QK = C_HEADS * C_DK
C_V = C_HEADS * C_DV
C_QKV = 2 * C_QK + C_V
SPLIT_SIZES = (A_QK, A_QK, A_V, A_V, B_QK, B_QK, B_V, GLA_RANK, B_V, C_QKV, C_HEADS, C_HEADS, C_V)
D_IN = sum(SPLIT_SIZES)

LANES = 128
SUBLANES = 8
VMEM_LIMIT_BYTES = 56 * 1024 * 1024

B_DKP = 64
B_DVP = 128
B_QKP = B_HEADS * B_DKP
B_VP = B_HEADS * B_DVP
SUB = 16
NSUB = CHUNK // SUB
PACK = 16
FSUB = 32
DECAY_CAP = 60.0
CHUNKS_PER_STEP = 4

OFF_QA = 0
OFF_FA = OFF_QA + A_QK
OFF_IA = OFF_FA + A_QK
OFF_ZA = OFF_IA + A_V
OFF_QB = OFF_ZA + A_V
OFF_KB = OFF_QB + B_QKP
OFF_VB = OFF_KB + B_QKP
OFF_ZB = OFF_VB + B_VP
OFF_QKVC = OFF_ZB + B_VP
OFF_ZC = OFF_QKVC + C_QKV
OFF_MISC = OFF_ZC + C_V
D_INP = OFF_MISC + LANES
MISC_LR = 0
MISC_AC = MISC_LR + GLA_RANK
MISC_BC = MISC_AC + C_HEADS
D_MIXP = A_V + B_VP + C_V
OFF_YA, OFF_YB, OFF_YC = 0, A_V, A_V + B_VP


def _padded_column_sources():
    offs = np.concatenate([[0], np.cumsum(SPLIT_SIZES)])
    (o_qa, o_fa, o_ia, o_za, o_qb, o_kb, o_vb, o_lr, o_zb, o_qkvc, o_ac, o_bc, o_zc) = offs[:-1]
    src = np.full((D_INP,), -1, np.int64)
    for dst, s0 in ((OFF_QA, o_qa), (OFF_FA, o_fa), (OFF_IA, o_ia), (OFF_ZA, o_za)):
        src[dst:dst + A_QK] = np.arange(s0, s0 + A_QK)
    for h in range(B_HEADS):
        src[OFF_QB + h * B_DKP:OFF_QB + h * B_DKP + B_DK] = np.arange(o_qb + h * B_DK, o_qb + (h + 1) * B_DK)
        src[OFF_KB + h * B_DKP:OFF_KB + h * B_DKP + B_DK] = np.arange(o_kb + h * B_DK, o_kb + (h + 1) * B_DK)
        src[OFF_VB + h * B_DVP:OFF_VB + h * B_DVP + B_DV] = np.arange(o_vb + h * B_DV, o_vb + (h + 1) * B_DV)
        src[OFF_ZB + h * B_DVP:OFF_ZB + h * B_DVP + B_DV] = np.arange(o_zb + h * B_DV, o_zb + (h + 1) * B_DV)
    src[OFF_QKVC:OFF_QKVC + C_QKV] = np.arange(o_qkvc, o_qkvc + C_QKV)
    src[OFF_ZC:OFF_ZC + C_V] = np.arange(o_zc, o_zc + C_V)
    src[OFF_MISC + MISC_LR:OFF_MISC + MISC_LR + GLA_RANK] = np.arange(o_lr, o_lr + GLA_RANK)
    src[OFF_MISC + MISC_AC:OFF_MISC + MISC_AC + C_HEADS] = np.arange(o_ac, o_ac + C_HEADS)
    src[OFF_MISC + MISC_BC:OFF_MISC + MISC_BC + C_HEADS] = np.arange(o_bc, o_bc + C_HEADS)
    return src


def _padded_mix_sources():
    src = np.full((D_MIXP,), -1, np.int64)
    src[OFF_YA:OFF_YA + A_V] = np.arange(0, A_V)
    for h in range(B_HEADS):
        src[OFF_YB + h * B_DVP:OFF_YB + h * B_DVP + B_DV] = np.arange(A_V + h * B_DV, A_V + (h + 1) * B_DV)
    src[OFF_YC:OFF_YC + C_V] = np.arange(A_V + B_V, A_V + B_V + C_V)
    return src


def _gather_padded(w, src, axis):
    idx = jnp.asarray(np.maximum(src, 0), jnp.int32)
    valid = jnp.asarray(src >= 0)
    shape = [1] * w.ndim
    shape[axis] = src.shape[0]
    return jnp.where(valid.reshape(shape), jnp.take(w, idx, axis=axis), 0.0)


def _block_ones(rows, rgroup, cols, cgroup):
    r = np.arange(rows)[:, None] // rgroup
    c = np.arange(cols)[None, :] // cgroup
    return (r == c).astype(np.float32)


def _head_select(nheads, width, group):
    m = np.zeros((max(nheads, SUBLANES), width), np.float32)
    for h in range(nheads):
        m[h, h * group:(h + 1) * group] = 1.0
    return m


def _level_masks():
    t = np.arange(CHUNK)[:, None]
    s = np.arange(CHUNK)[None, :]
    masks = []
    m = 1
    while m < CHUNK:
        masks.append(((t // (2 * m) == s // (2 * m)) & (t % (2 * m) >= m) & (s % (2 * m) < m)).astype(np.float32))
        m *= 2
    return np.stack(masks)


def _packed_masks():
    lvl = _level_masks()
    nlev = int(np.log2(PACK))
    lvlp = np.stack([np.tile(lvl[lev][0:PACK, 0:PACK], (1, CHUNK // PACK)) for lev in range(nlev)])
    lane_block = np.arange(CHUNK)[None, :] // PACK
    pick = np.stack([np.broadcast_to(lane_block == b, (PACK, CHUNK)) for b in range(CHUNK // PACK)])
    bdiag = (np.arange(CHUNK)[:, None] // PACK == lane_block)
    return lvlp.astype(np.float32), pick.astype(np.float32), bdiag.astype(np.float32)


def _constants():
    lvlp, pick, bdiag = _packed_masks()
    ltri = np.tril(np.ones((CHUNK, CHUNK), np.float32))
    e_a = np.zeros((LANES, C_QK), np.float32)
    e_b = np.zeros((LANES, C_QK), np.float32)
    for h in range(C_HEADS):
        e_a[MISC_AC + h, h * C_DK:(h + 1) * C_DK] = 1.0
        e_b[MISC_BC + h, h * C_DK:(h + 1) * C_DK] = 1.0
    return dict(
        ltri=jnp.asarray(ltri, BF16),
        ltri_g=jnp.asarray(np.kron(np.eye(CHUNKS_PER_STEP, dtype=np.float32), ltri), BF16),
        e_a=jnp.asarray(e_a, BF16),
        e_b=jnp.asarray(e_b, BF16),
        ones_a=jnp.asarray(_block_ones(A_QK, A_DK, A_V, A_DV), BF16),
        ones_b=jnp.asarray(_block_ones(B_QKP, B_DKP, B_VP, B_DVP), BF16),
        ones_bv=jnp.asarray(_block_ones(B_VP, B_DVP, B_VP, B_DVP), BF16),
        ones_c=jnp.asarray(_block_ones(C_QK, C_DK, C_QK, C_DK), BF16),
        mask_a=jnp.asarray(_block_ones(A_V, A_DV, A_QK, A_DK), F32),
        mask_b=jnp.asarray(_block_ones(B_VP, B_DVP, B_QKP, B_DKP), F32),
        hmk=jnp.asarray(_head_select(A_HEADS, A_QK, A_DK), F32),
        hmv_a=jnp.asarray(_head_select(A_HEADS, A_V, A_DV), F32),
        hmv_b=jnp.asarray(_head_select(B_HEADS, B_VP, B_DVP), F32),
        lvl=jnp.asarray(_level_masks(), BF16),
        lvlp=jnp.asarray(lvlp, BF16),
        pick=jnp.asarray(pick, BF16),
        bdiag=jnp.asarray(bdiag, BF16),
    )


def _sigmoid(x):
    return 1.0 / (1.0 + jnp.exp(-x))


def _softplus(x):
    return jnp.maximum(x, 0.0) + jnp.log1p(jnp.exp(-jnp.abs(x)))


def _split_bf16(a):
    hi = a.astype(BF16)
    lo = (a - hi.astype(F32)).astype(BF16)
    return hi, lo


def _dot(a, b):
    return jnp.dot(a, b, preferred_element_type=F32)


def _dot_nt(a, b):
    return lax.dot_general(a, b, (((1,), (1,)), ((), ())), preferred_element_type=F32)


def _dot_tn(a, b):
    return lax.dot_general(a, b, (((0,), (0,)), ((), ())), preferred_element_type=F32)


def _dot_lhs2(a, b_bf16):
    hi, lo = _split_bf16(a)
    both = _dot(jnp.concatenate([hi, lo], axis=0), b_bf16)
    return both[0:a.shape[0]] + both[a.shape[0]:]


def _dot_rhs2(a_bf16, b):
    hi, lo = _split_bf16(b)
    return _dot(a_bf16, hi) + _dot(a_bf16, lo)


def _dot3(a, b):
    ah, al = _split_bf16(a)
    bh, bl = _split_bf16(b)
    return _dot(ah, bh) + (_dot(ah, bl) + _dot(al, bh))


def _layer_norm(x):
    mu = jnp.mean(x, axis=-1, keepdims=True)
    xc = x - mu
    var = jnp.mean(xc * xc, axis=-1, keepdims=True)
    return xc * lax.rsqrt(var + LN_EPS)


def _block_decay_span(bcum):
    spans = [bcum[n:n + 1] - bcum[n + FSUB - 1:n + FSUB] for n in range(0, CHUNK, FSUB)]
    return functools.reduce(jnp.maximum, spans)


def _gla_intra_bounded(items, hmk):
    pairs = [(c, i) for c in range(len(items)) for i in range(CHUNK // FSUB)]
    qstacks, kscs = [], []
    for c, i in pairs:
        q, k, _, bcum, _, nheads = items[c]
        n, m = i * FSUB, (i + 1) * FSUB
        ri = bcum[n:n + 1]
        qt = q[n:m] * jnp.exp(bcum[n:m] - ri)
        kscs.append((k[0:m] * jnp.exp(ri - bcum[0:m])).astype(BF16))
        qstacks.append(jnp.concatenate([qt * hmk[h:h + 1] for h in range(nheads)], axis=0).astype(BF16))
        yield
    ssts = [_dot_nt(qs, ks) for qs, ks in zip(qstacks, kscs)]
    yield
    masked = []
    for (c, i), sst in zip(pairs, ssts):
        t_io = lax.broadcasted_iota(jnp.int32, sst.shape, 0) % FSUB + i * FSUB
        s_io = lax.broadcasted_iota(jnp.int32, sst.shape, 1)
        masked.append(jnp.where(t_io >= s_io, sst, 0.0).astype(BF16))
    rs = [_dot(ms, items[c][2][0:(i + 1) * FSUB]) for (c, i), ms in zip(pairs, masked)]
    yield
    blocks = []
    for (c, i), r in zip(pairs, rs):
        hmv, nheads = items[c][4], items[c][5]
        o_i = r[0:FSUB] * hmv[0:1]
        for h in range(1, nheads):
            o_i = o_i + r[h * FSUB:(h + 1) * FSUB] * hmv[h:h + 1]
        blocks.append(o_i)
        yield
    per = CHUNK // FSUB
    return [jnp.concatenate(blocks[c * per:(c + 1) * per], axis=0) for c in range(len(items))]


def _gla_intra_direct(q, k, v, v16, bcum, ones_kv, hmk, hmv, nheads):
    tio = lax.broadcasted_iota(jnp.int32, (SUB, q.shape[1]), 0)
    outs = []
    for i in range(NSUB):
        n = i * SUB
        qi = q[n:n + SUB]
        ki = k[n:n + SUB]
        bi = bcum[n:n + SUB]
        vi = v[n:n + SUB]
        xs = []
        for s in range(SUB):
            rel = jnp.minimum(bi - bi[s:s + 1], 0.0)
            x = qi * ki[s:s + 1] * jnp.exp(rel)
            xs.append(jnp.where(tio >= s, x, 0.0))
        x_all = jnp.concatenate(xs, axis=0).astype(BF16)
        r_all = _dot(x_all, ones_kv)
        o_i = r_all[0:SUB] * vi[0:1]
        for s in range(1, SUB):
            o_i = o_i + r_all[s * SUB:(s + 1) * SUB] * vi[s:s + 1]
        if i > 0:
            ri = bcum[n:n + 1]
            ksc = (k[0:n] * jnp.exp(ri - bcum[0:n])).astype(BF16)
            qt = qi * jnp.exp(bi - ri)
            qstack = jnp.concatenate([qt * hmk[h:h + 1] for h in range(nheads)], axis=0).astype(BF16)
            sst = _dot_nt(qstack, ksc)
            r_od = _dot(sst.astype(BF16), v16[0:n])
            for h in range(nheads):
                o_i = o_i + r_od[h * SUB:(h + 1) * SUB] * hmv[h:h + 1]
        outs.append(o_i)
    return jnp.concatenate(outs, axis=0)


def _gla_streams(streams, hmk):
    flat = [(si, c) for si, st in enumerate(streams) for c in range(len(st["chunks"]))]
    pre = []
    for si, c in flat:
        q, k, v, bcum = streams[si]["chunks"][c]
        b_last = bcum[CHUNK - 1:CHUNK, :]
        pre.append(dict(v16=v.astype(BF16), qe16=(q * jnp.exp(bcum)).astype(BF16),
                        kdec16=(k * jnp.exp(b_last - bcum)).astype(BF16), d=jnp.exp(b_last)))
        yield
    upds = [_dot_tn(p["v16"], p["kdec16"]) for p in pre]
    yield
    intra = yield from _gla_intra_bounded(
        [streams[si]["chunks"][c][0:2] + (p["v16"], streams[si]["chunks"][c][3], streams[si]["hmv"],
                                         streams[si]["nheads"]) for (si, c), p in zip(flat, pre)], hmk)
    inters = [[] for _ in streams]
    for si, st in enumerate(streams):
        state = st["st_ref"][...]
        for c in range(len(st["chunks"])):
            j = flat.index((si, c))
            inters[si].append(_dot_nt(pre[j]["qe16"], state.astype(BF16)))
            state = state * pre[j]["d"] + upds[j] * st["st_mask"]
            yield
        st["st_ref"][...] = state
    intras = [[intra[flat.index((si, c))] for c in range(len(st["chunks"]))] for si, st in enumerate(streams)]
    return inters, intras


def _unit_lower_inverses(nmats, lvl_ref, lvlp_ref, pick_ref, bdiag_ref):
    r = lax.broadcasted_iota(jnp.int32, (CHUNK, CHUNK), 0)
    c = lax.broadcasted_iota(jnp.int32, (CHUNK, CHUNK), 1)
    eye = jnp.where(r == c, 1.0, 0.0)
    nblk = CHUNK // PACK
    bdiag = bdiag_ref[...]

    def pack(x):
        return functools.reduce(lambda a, b: a + b,
                                [x[b * PACK:(b + 1) * PACK] * pick_ref[b] for b in range(nblk)])

    def unpack(xp):
        return jnp.concatenate([xp] * nblk, axis=0) * bdiag

    negs = [(-n).astype(BF16) for n in nmats]
    fulls = [(eye - n * lvl_ref[0].astype(F32)).astype(BF16) for n in nmats]
    negps = [pack(n16) for n16 in negs]
    tps = [pack(t) for t in fulls]
    for lev in range(1, lvlp_ref.shape[0]):
        pps = [_dot(np16, t).astype(BF16) for np16, t in zip(negps, fulls)]
        yield
        gps = [_dot(tp, unpack(pp)).astype(BF16) for tp, pp in zip(tps, pps)]
        yield
        tps = [tp + gp * lvlp_ref[lev] for tp, gp in zip(tps, gps)]
        fulls = [unpack(tp) for tp in tps]
    m = PACK
    lev = lvlp_ref.shape[0]
    while m < CHUNK:
        rows = [(lo, lo + m) for lo in range(m, CHUNK, 2 * m)]
        keep = [(lo - m, lo) for lo in range(m, CHUNK, 2 * m)]

        def take(x):
            return jnp.concatenate([x[a:b] for a, b in rows], axis=0)

        mask_r = take(lvl_ref[lev])
        zeros = jnp.zeros((m, CHUNK), BF16)
        prs = [_dot(take(n16), t).astype(BF16) for n16, t in zip(negs, fulls)]
        yield
        pembs = [jnp.concatenate([piece for i in range(len(rows)) for piece in (zeros, pr[i * m:(i + 1) * m])],
                                 axis=0) for pr in prs]
        grs = [_dot(take(t), pe).astype(BF16) for t, pe in zip(fulls, pembs)]
        yield
        news = [take(t) + gr * mask_r for t, gr in zip(fulls, grs)]
        fulls = [jnp.concatenate([piece for i, (a, b) in enumerate(keep)
                                  for piece in (t[a:b], new[i * m:(i + 1) * m])], axis=0)
                 for t, new in zip(fulls, news)]
        m *= 2
        lev += 1
    return fulls


def _interleave(*gens_and_steps):
    live = list(gens_and_steps)
    while live:
        for entry in list(live):
            gen, steps = entry
            try:
                for _ in range(steps):
                    next(gen)
            except StopIteration:
                live.remove(entry)


def _layer_kernel(x_ref, mod_ref, win_ref, wout_ref, lng_ref, lnb_ref, lb_ref, wgkh_ref, wgkl_ref, bgk_ref,
                  convw_ref, avec_ref, dtvec_ref, gain_ref,
                  ltri_ref, ltrig_ref, ea_ref, eb_ref, ones_a_ref, ones_b_ref, ones_bv_ref, ones_c_ref,
                  mask_a_ref, mask_b_ref, hmk_ref, hmva_ref, hmvb_ref, lvl_ref, lvlp_ref, pick_ref, bdiag_ref,
                  out_ref,
                  proj_ref, y_ref, ointer_ref, sa_ref, sb_ref, sc_ref, conv_ref):
    tile = x_ref.shape[1]

    @pl.when(pl.program_id(1) == 0)
    def _():
        sa_ref[...] = jnp.zeros_like(sa_ref)
        sb_ref[...] = jnp.zeros_like(sb_ref)
        sc_ref[...] = jnp.zeros_like(sc_ref)
        conv_ref[...] = jnp.zeros_like(conv_ref)

    x = x_ref[0]
    shift = mod_ref[0, 0:1, :]
    scale = mod_ref[0, 1:2, :]
    gate = mod_ref[0, 2:3, :]
    h = _layer_norm(x) * (1.0 + scale) + shift
    proj_ref[...] = _dot(h.astype(BF16), win_ref[...])

    ltri = ltri_ref[...]
    hmk = hmk_ref[...]
    hmva = hmva_ref[...]
    hmvb = hmvb_ref[...]
    t_io = lax.broadcasted_iota(jnp.int32, (CHUNK, CHUNK), 0)
    s_io = lax.broadcasted_iota(jnp.int32, (CHUNK, CHUNK), 1)
    causal = t_io >= s_io
    strict = t_io > s_io

    def norm_gate(o, z, ones, dv, gain):
        ms = _dot((o * o).astype(BF16), ones) * (1.0 / dv)
        return (o * lax.rsqrt(ms + NORM_EPS) * gain * (z * _sigmoid(z))).astype(BF16)

    def group_body(g, carry):
        chunk_rows = [pl.ds(pl.multiple_of((g * CHUNKS_PER_STEP + i) * CHUNK, CHUNK), CHUNK)
                      for i in range(CHUNKS_PER_STEP)]

        group_rows = pl.ds(pl.multiple_of(g * (CHUNKS_PER_STEP * CHUNK), CHUNKS_PER_STEP * CHUNK),
                           CHUNKS_PER_STEP * CHUNK)

        def chunks_of(arr):
            return [arr[i * CHUNK:(i + 1) * CHUNK] for i in range(CHUNKS_PER_STEP)]

        def ab_inputs(rows, ltri_rows):
            qa = proj_ref[rows, OFF_QA:OFF_QA + A_QK]
            fa = proj_ref[rows, OFF_FA:OFF_FA + A_QK]
            lbv = lb_ref[...]
            sg = _sigmoid(fa)
            f_a = lbv + (1.0 - lbv) * sg
            g_a = jnp.log(jnp.maximum(f_a, FORGET_FLOOR))
            k_a = (1.0 - lbv) * (1.0 - sg)
            q_a = qa * _sigmoid(qa)
            misc = proj_ref[rows, OFF_MISC:OFF_MISC + LANES]
            m_hi, m_lo = _split_bf16(misc)
            n = misc.shape[0]
            gk_hi = _dot(jnp.concatenate([m_hi, m_lo], axis=0), wgkh_ref[...])
            gk_lin = gk_hi[0:n] + (gk_hi[n:] + _dot(m_hi, wgkl_ref[...])) + bgk_ref[...]
            g_b = -_softplus(-gk_lin) * (1.0 / GLA_GATE_TEMP)
            in_a = (q_a, k_a, proj_ref[rows, OFF_IA:OFF_IA + A_V], _dot_rhs2(ltri_rows, g_a))
            in_b = (proj_ref[rows, OFF_QB:OFF_QB + B_QKP] * (B_DK ** -0.5), proj_ref[rows, OFF_KB:OFF_KB + B_QKP],
                    proj_ref[rows, OFF_VB:OFF_VB + B_VP], _dot_rhs2(ltri_rows, g_b))
            return in_a, in_b, misc

        def store_y_ab(rows, o_a, o_b):
            za = proj_ref[rows, OFF_ZA:OFF_ZA + A_V]
            y_ref[rows, OFF_YA:OFF_YA + A_V] = norm_gate(o_a, za, ones_a_ref[...], A_DV,
                                                         gain_ref[:, OFF_YA:OFF_YA + A_V])
            zb = proj_ref[rows, OFF_ZB:OFF_ZB + B_VP]
            y_ref[rows, OFF_YB:OFF_YB + B_VP] = norm_gate(o_b, zb, ones_bv_ref[...], B_DV,
                                                          gain_ref[:, OFF_YB:OFF_YB + B_VP])

        in_a, in_b, misc_g = ab_inputs(group_rows, ltrig_ref[...])
        chunks_a = list(zip(*[chunks_of(t) for t in in_a]))
        chunks_b = list(zip(*[chunks_of(t) for t in in_b]))
        stream_a = dict(chunks=chunks_a, st_ref=sa_ref, st_mask=mask_a_ref[...], hmv=hmva, nheads=A_HEADS)
        stream_b = dict(chunks=chunks_b, st_ref=sb_ref, st_mask=mask_b_ref[...], hmv=hmvb, nheads=B_HEADS)

        def ab_work():
            (inter_a, inter_b), (intra_a, intra_b) = yield from _gla_streams([stream_a, stream_b], hmk)
            inter_a, inter_b = jnp.concatenate(inter_a, axis=0), jnp.concatenate(inter_b, axis=0)
            ointer_ref[group_rows, 0:A_V] = inter_a
            ointer_ref[group_rows, A_V:A_V + B_VP] = inter_b
            yield
            store_y_ab(group_rows, inter_a + jnp.concatenate(intra_a, axis=0),
                       inter_b + jnp.concatenate(intra_b, axis=0))
            yield

        span = functools.reduce(jnp.maximum, [_block_decay_span(c[3]) for c in chunks_a + chunks_b])
        unbounded = jnp.logical_not(jnp.max(span) <= DECAY_CAP)

        ones_c = ones_c_ref[...]
        cw = convw_ref[...]
        glen = CHUNKS_PER_STEP * CHUNK
        raw = proj_ref[group_rows, OFF_QKVC:OFF_QKVC + C_QKV]
        win = jnp.concatenate([conv_ref[...], raw], axis=0)
        conv_ref[...] = raw[glen - SUBLANES:glen]
        back1 = pltpu.roll(win, 1, axis=0)
        older = pltpu.roll(win * cw[1:2] + back1 * cw[0:1], 2, axis=0)
        acc = (win * cw[3:4] + back1 * cw[2:3] + older)[SUBLANES:SUBLANES + glen]
        qkv = acc * _sigmoid(acc)
        q_c = qkv[:, 0:C_QK]
        k_c = qkv[:, C_QK:2 * C_QK]
        v_c = qkv[:, 2 * C_QK:C_QKV]
        ssq = _dot(jnp.concatenate([q_c * q_c, k_c * k_c], axis=0).astype(BF16), ones_c)
        q_c = q_c * lax.rsqrt(ssq[0:glen] + NORM_EPS) * (C_DK ** -0.5)
        k_c = k_c * lax.rsqrt(ssq[glen:] + NORM_EPS)
        log_a = -jnp.exp(avec_ref[...]) * _softplus(misc_g + dtvec_ref[...])
        beta = _sigmoid(misc_g)
        bc_small = _dot_rhs2(ltrig_ref[...], log_a)
        bexp = _dot_lhs2(bc_small, ea_ref[...])
        beta_x = _dot_lhs2(beta, eb_ref[...])
        e_b = jnp.exp(bexp)
        bl_rows = [bexp[(i + 1) * CHUNK - 1:(i + 1) * CHUNK, :] for i in range(CHUNKS_PER_STEP)]
        bl_g = jnp.concatenate([jnp.broadcast_to(bl, (CHUNK, C_QK)) for bl in bl_rows], axis=0)
        kb_c = k_c * beta_x
        whole = dict(bexp=bexp, q16=q_c.astype(BF16), k16=k_c.astype(BF16), kb16=kb_c.astype(BF16),
                     vb16=(v_c * beta_x).astype(BF16), kbe16=(kb_c * e_b).astype(BF16),
                     qe16=(q_c * e_b).astype(BF16), kdl16=(k_c * jnp.exp(bl_g - bexp)).astype(BF16))
        prep = [dict({name: arr[i * CHUNK:(i + 1) * CHUNK] for name, arr in whole.items()},
                     b_rows=bc_small[i * CHUNK:(i + 1) * CHUNK].T, e_last=jnp.exp(bl_rows[i]))
                for i in range(CHUNKS_PER_STEP)]

        items = [(i, hd) for i in range(CHUNKS_PER_STEP) for hd in range(C_HEADS)]

        def hsl(hd):
            return slice(hd * C_DK, (hd + 1) * C_DK)

        kks = [_dot_nt(prep[i]["kb16"][:, hsl(hd)], prep[i]["k16"][:, hsl(hd)]) for i, hd in items]
        qks = [_dot_nt(prep[i]["q16"][:, hsl(hd)], prep[i]["k16"][:, hsl(hd)]) for i, hd in items]
        decs = [jnp.exp(jnp.minimum(
            prep[i]["bexp"][:, hsl(hd)] - prep[i]["b_rows"][MISC_AC + hd:MISC_AC + hd + 1, :], 0.0))
            for i, hd in items]
        nmats = [jnp.where(strict, kk * dec, 0.0) for kk, dec in zip(kks, decs)]
        attns = [jnp.where(causal, qk * dec, 0.0).astype(BF16) for qk, dec in zip(qks, decs)]
        t16s = []

        def inverse_work():
            t16s.extend((yield from _unit_lower_inverses(nmats, lvl_ref, lvlp_ref, pick_ref, bdiag_ref)))

        _interleave((inverse_work(), 1), (ab_work(), 3))
        us = [_dot(t16, prep[i]["vb16"][:, hsl(hd)]) for t16, (i, hd) in zip(t16s, items)]
        ws = [_dot(t16, prep[i]["kbe16"][:, hsl(hd)]).astype(BF16) for t16, (i, hd) in zip(t16s, items)]
        u16s = [u.astype(BF16) for u in us]
        kws = [_dot_tn(prep[i]["kdl16"][:, hsl(hd)], w).astype(BF16) for w, (i, hd) in zip(ws, items)]
        kus = [_dot_tn(prep[i]["kdl16"][:, hsl(hd)], u16) for u16, (i, hd) in zip(u16s, items)]

        states = [sc_ref[hd] for hd in range(C_HEADS)]
        s16s = []
        for i in range(CHUNKS_PER_STEP):
            base = i * C_HEADS
            cur16 = [s.astype(BF16) for s in states]
            s16s.extend(cur16)
            moved = [_dot(kws[base + hd], cur16[hd]) for hd in range(C_HEADS)]
            states = [states[hd] * prep[i]["e_last"][:, hsl(hd)] - moved[hd] + kus[base + hd]
                      for hd in range(C_HEADS)]
        for hd in range(C_HEADS):
            sc_ref[hd] = states[hd]

        ws_os = [_dot(jnp.concatenate([w, prep[i]["qe16"][:, hsl(hd)]], axis=0), s16)
                 for w, s16, (i, hd) in zip(ws, s16s, items)]
        vn16s = [(u - wo[0:CHUNK]).astype(BF16) for u, wo in zip(us, ws_os)]
        o_items = [wo[CHUNK:2 * CHUNK] + _dot(attn, vn16) for wo, attn, vn16 in zip(ws_os, attns, vn16s)]
        o_c = jnp.concatenate([jnp.concatenate(o_items[i * C_HEADS:(i + 1) * C_HEADS], axis=1)
                               for i in range(CHUNKS_PER_STEP)], axis=0)
        zc = proj_ref[group_rows, OFF_ZC:OFF_ZC + C_V]
        y_ref[group_rows, OFF_YC:OFF_YC + C_V] = norm_gate(o_c, zc, ones_c, C_DV, gain_ref[:, OFF_YC:OFF_YC + C_V])

        @pl.when(unbounded)
        def _():
            for rows in chunk_rows:
                (q_a, k_a, v_a, bcum_a), (q_b, k_b, v_b, bcum_b), _ = ab_inputs(rows, ltri)
                o_a = _gla_intra_direct(q_a, k_a, v_a, v_a.astype(BF16), bcum_a, ones_a_ref[...], hmk, hmva,
                                        A_HEADS)
                o_b = _gla_intra_direct(q_b, k_b, v_b, v_b.astype(BF16), bcum_b, ones_b_ref[...], hmk, hmvb,
                                        B_HEADS)
                store_y_ab(rows, ointer_ref[rows, 0:A_V] + o_a, ointer_ref[rows, A_V:A_V + B_VP] + o_b)
        return carry

    lax.fori_loop(0, tile // (CHUNK * CHUNKS_PER_STEP), group_body, 0)

    out = _dot(y_ref[...], wout_ref[...])
    res = DEEPNORM_ALPHA * x + gate * out
    out_ref[0] = _layer_norm(res) * lng_ref[...] + lnb_ref[...]


def _seq_tile(seq):
    tile = CHUNK * CHUNKS_PER_STEP
    assert seq % tile == 0
    while tile * 2 <= min(seq, 512) and seq % (tile * 2) == 0:
        tile *= 2
    return tile


def _const_spec(arr):
    nd = arr.ndim
    return pl.BlockSpec(arr.shape, lambda b, j, _nd=nd: (0,) * _nd)


def _hybrid_layer_call(x, mod, w_in_p, w_out_p, ln_g, ln_b, lbv, wgk_hi, wgk_lo, bgk, conv_w, avec, dtvec, gains,
                       consts):
    bsz, seq, d = x.shape
    assert d == D_MODEL and seq % CHUNK == 0
    tile = _seq_tile(seq)
    names = ("ltri", "ltri_g", "e_a", "e_b", "ones_a", "ones_b", "ones_bv", "ones_c", "mask_a", "mask_b",
             "hmk", "hmv_a", "hmv_b", "lvl", "lvlp", "pick", "bdiag")
    small = (w_in_p, w_out_p, ln_g, ln_b, lbv, wgk_hi, wgk_lo, bgk, conv_w, avec, dtvec, gains) + tuple(
        consts[n] for n in names)
    in_specs = [
        pl.BlockSpec((1, tile, D_MODEL), lambda b, j: (b, j, 0)),
        pl.BlockSpec((1, 3, D_MODEL), lambda b, j: (b, 0, 0)),
    ] + [_const_spec(a) for a in small]
    return pl.pallas_call(
        _layer_kernel,
        grid=(bsz, seq // tile),
        in_specs=in_specs,
        out_specs=pl.BlockSpec((1, tile, D_MODEL), lambda b, j: (b, j, 0)),
        out_shape=jax.ShapeDtypeStruct((bsz, seq, D_MODEL), F32),
        scratch_shapes=[
            pltpu.VMEM((tile, D_INP), F32),
            pltpu.VMEM((tile, D_MIXP), BF16),
            pltpu.VMEM((tile, A_V + B_VP), F32),
            pltpu.VMEM((A_V, A_QK), F32),
            pltpu.VMEM((B_VP, B_QKP), F32),
            pltpu.VMEM((C_HEADS, C_DK, C_DV), F32),
            pltpu.VMEM((SUBLANES, C_QKV), F32),
        ],
        compiler_params=pltpu.CompilerParams(
            dimension_semantics=("arbitrary", "arbitrary"),
            vmem_limit_bytes=VMEM_LIMIT_BYTES),
        name="hybrid_layer",
    )(x, mod, *small)


def _mod_kernel(c_ref, w_ref, b_ref, o_ref):
    c = c_ref[...]
    c_act = c * _sigmoid(c)
    o_ref[0] = _dot3(c_act, w_ref[0]) + b_ref[0]


def _ada_mod_call(c, ada_w, ada_b):
    depth, d, d3 = ada_w.shape
    bsz = c.shape[0]
    ncol = 512
    assert d3 % ncol == 0
    return pl.pallas_call(
        _mod_kernel,
        grid=(depth, d3 // ncol),
        in_specs=[
            pl.BlockSpec((bsz, d), lambda l, n: (0, 0)),
            pl.BlockSpec((1, d, ncol), lambda l, n: (l, 0, n)),
            pl.BlockSpec((1, 1, ncol), lambda l, n: (l, 0, n)),
        ],
        out_specs=pl.BlockSpec((1, bsz, ncol), lambda l, n: (l, 0, n)),
        out_shape=jax.ShapeDtypeStruct((depth, bsz, d3), F32),
        compiler_params=pltpu.CompilerParams(dimension_semantics=("arbitrary", "arbitrary")),
        name="ada_mod",
    )(c, ada_w, ada_b.reshape(depth, 1, d3))


def _lb_kernel(logit_ref, o_ref):
    depth = logit_ref.shape[0]
    rows = [logit_ref[l:l + 1, :] for l in range(depth)]
    mx = functools.reduce(jnp.maximum, rows)
    ex = [jnp.exp(r - mx) for r in rows]
    inv = 1.0 / functools.reduce(lambda a, b: a + b, ex)
    p = [e * inv for e in ex]
    run = jnp.zeros_like(p[0])
    for l in range(depth):
        run = run + p[l]
        o_ref[l:l + 1, :] = run - p[0]


def _lb_table_call(logits):
    return pl.pallas_call(
        _lb_kernel,
        out_shape=jax.ShapeDtypeStruct(logits.shape, F32),
        name="hgrn_lb",
    )(logits)


def kernel(x, c, w_in, w_out, ada_w, ada_b, ln_g, ln_b, hgrn_lb_logits, gla_w_gk, gla_b_gk, gdn_conv_w, gdn_a_log,
           gdn_dt_bias, gain_a, gain_b, gain_c):
    depth = w_in.shape[0]
    bsz = x.shape[0]
    consts = _constants()
    col_src = _padded_column_sources()
    mix_src = _padded_mix_sources()
    gk_src = np.full((B_QKP,), -1, np.int64)
    for h in range(B_HEADS):
        gk_src[h * B_DKP:h * B_DKP + B_DK] = np.arange(h * B_DK, (h + 1) * B_DK)

    mod_all = _ada_mod_call(c, ada_w, ada_b).reshape(depth, bsz, 3, D_MODEL)
    lb_table = _lb_table_call(hgrn_lb_logits.astype(F32))

    for l in range(depth):
        w_in_p = _gather_padded(w_in[l], col_src, 1).astype(BF16)
        w_out_p = _gather_padded(w_out[l], mix_src, 0).astype(BF16)
        wgk = _gather_padded(gla_w_gk[l].astype(F32), gk_src, 1)
        wgk = jnp.zeros((LANES, B_QKP), F32).at[MISC_LR:MISC_LR + GLA_RANK].set(wgk)
        wgk_hi = wgk.astype(BF16)
        wgk_lo = (wgk - wgk_hi.astype(F32)).astype(BF16)
        bgk = _gather_padded(gla_b_gk[l].astype(F32), gk_src, 0).reshape(1, B_QKP)
        avec = jnp.zeros((1, LANES), F32).at[0, MISC_AC:MISC_AC + C_HEADS].set(gdn_a_log[l].astype(F32))
        dtvec = jnp.zeros((1, LANES), F32).at[0, MISC_AC:MISC_AC + C_HEADS].set(gdn_dt_bias[l].astype(F32))
        gain_b_p = jnp.zeros((B_DVP,), F32).at[:B_DV].set(gain_b[l].astype(F32))
        gains = jnp.concatenate([jnp.tile(gain_a[l].astype(F32), A_HEADS), jnp.tile(gain_b_p, B_HEADS),
                                 jnp.tile(gain_c[l].astype(F32), C_HEADS)]).reshape(1, D_MIXP)
        x = _hybrid_layer_call(
            x, mod_all[l], w_in_p, w_out_p, ln_g[l].reshape(1, D_MODEL), ln_b[l].reshape(1, D_MODEL),
            lb_table[l].reshape(1, A_QK), wgk_hi, wgk_lo, bgk, gdn_conv_w[l].astype(F32), avec, dtvec, gains, consts)
    return x
```

```python
import functools

import numpy as np
import jax
import jax.numpy as jnp
from jax import lax
from jax.experimental import pallas as pl
from jax.experimental.pallas import tpu as pltpu

F32 = jnp.float32
BF16 = jnp.bfloat16

D_MODEL = 1024
DEPTH = 2
CHUNK = 64
CONV_WIDTH = 4
assert CONV_WIDTH == 4
A_HEADS, A_DK, A_DV = 4, 64, 64
B_HEADS, B_DK, B_DV = 4, 48, 96
GLA_RANK = 16
GLA_GATE_TEMP = 16.0
C_HEADS, C_DK, C_DV = 6, 64, 64
DEEPNORM_ALPHA = (2 * DEPTH) ** 0.25
LN_EPS = 1e-5
NORM_EPS = 1e-6
FORGET_FLOOR = 1e-30

A_QK = A_HEADS * A_DK
A_V = A_HEADS * A_DV
B_QK = B_HEADS * B_DK
B_V = B_HEADS * B_DV
C_QK = C_HEADS * C_DK
C_V = C_HEADS * C_DV
C_QKV = 2 * C_QK + C_V
SPLIT_SIZES = (A_QK, A_QK, A_V, A_V, B_QK, B_QK, B_V, GLA_RANK, B_V, C_QKV, C_HEADS, C_HEADS, C_V)
D_IN = sum(SPLIT_SIZES)

LANES = 128
SUBLANES = 8
VMEM_LIMIT_BYTES = 56 * 1024 * 1024

B_DKP = B_DK
B_DVP = B_DV
B_QKP = B_HEADS * B_DKP
B_VP = B_HEADS * B_DVP
B_QK_SLOT = 256
SUB = 16
NSUB = CHUNK // SUB
PACK = 16
FSUB = 32
DECAY_CAP = 60.0
CHUNKS_PER_STEP = 4

OFF_QA = 0
OFF_FA = OFF_QA + A_QK
OFF_IA = OFF_FA + A_QK
OFF_ZA = OFF_IA + A_V
OFF_QB = OFF_ZA + A_V
OFF_KB = OFF_QB + B_QK_SLOT
OFF_VB = OFF_KB + B_QK_SLOT
OFF_ZB = OFF_VB + B_VP
OFF_QKVC = OFF_ZB + B_VP
OFF_ZC = OFF_QKVC + C_QKV
D_INP = OFF_ZC + C_V
OFF_MISC = OFF_QB + B_QK_SLOT - LANES
MISC_LR = B_QKP - (B_QK_SLOT - LANES)
MISC_AC = MISC_LR + GLA_RANK
MISC_BC = MISC_AC + C_HEADS
assert MISC_BC + C_HEADS <= LANES and D_INP % LANES == 0
D_MIXP = A_V + B_VP + C_V
OFF_YA, OFF_YB, OFF_YC = 0, A_V, A_V + B_VP


def _padded_column_sources():
    offs = np.concatenate([[0], np.cumsum(SPLIT_SIZES)])
    (o_qa, o_fa, o_ia, o_za, o_qb, o_kb, o_vb, o_lr, o_zb, o_qkvc, o_ac, o_bc, o_zc) = offs[:-1]
    src = np.full((D_INP,), -1, np.int64)
    for dst, s0 in ((OFF_QA, o_qa), (OFF_FA, o_fa), (OFF_IA, o_ia), (OFF_ZA, o_za)):
        src[dst:dst + A_QK] = np.arange(s0, s0 + A_QK)
    for h in range(B_HEADS):
        src[OFF_QB + h * B_DKP:OFF_QB + h * B_DKP + B_DK] = np.arange(o_qb + h * B_DK, o_qb + (h + 1) * B_DK)
        src[OFF_KB + h * B_DKP:OFF_KB + h * B_DKP + B_DK] = np.arange(o_kb + h * B_DK, o_kb + (h + 1) * B_DK)
        src[OFF_VB + h * B_DVP:OFF_VB + h * B_DVP + B_DV] = np.arange(o_vb + h * B_DV, o_vb + (h + 1) * B_DV)
        src[OFF_ZB + h * B_DVP:OFF_ZB + h * B_DVP + B_DV] = np.arange(o_zb + h * B_DV, o_zb + (h + 1) * B_DV)
    src[OFF_QKVC:OFF_QKVC + C_QKV] = np.arange(o_qkvc, o_qkvc + C_QKV)
    src[OFF_ZC:OFF_ZC + C_V] = np.arange(o_zc, o_zc + C_V)
    src[OFF_MISC + MISC_LR:OFF_MISC + MISC_LR + GLA_RANK] = np.arange(o_lr, o_lr + GLA_RANK)
    src[OFF_MISC + MISC_AC:OFF_MISC + MISC_AC + C_HEADS] = np.arange(o_ac, o_ac + C_HEADS)
    src[OFF_MISC + MISC_BC:OFF_MISC + MISC_BC + C_HEADS] = np.arange(o_bc, o_bc + C_HEADS)
    return src


def _padded_mix_sources():
    src = np.full((D_MIXP,), -1, np.int64)
    src[OFF_YA:OFF_YA + A_V] = np.arange(0, A_V)
    for h in range(B_HEADS):
        src[OFF_YB + h * B_DVP:OFF_YB + h * B_DVP + B_DV] = np.arange(A_V + h * B_DV, A_V + (h + 1) * B_DV)
    src[OFF_YC:OFF_YC + C_V] = np.arange(A_V + B_V, A_V + B_V + C_V)
    return src


def _gather_padded(w, src, axis):
    idx = jnp.asarray(np.maximum(src, 0), jnp.int32)
    valid = jnp.asarray(src >= 0)
    shape = [1] * w.ndim
    shape[axis] = src.shape[0]
    return jnp.where(valid.reshape(shape), jnp.take(w, idx, axis=axis), 0.0)


def _block_ones(rows, rgroup, cols, cgroup):
    r = np.arange(rows)[:, None] // rgroup
    c = np.arange(cols)[None, :] // cgroup
    return (r == c).astype(np.float32)


def _head_select(nheads, width, group):
    m = np.zeros((max(nheads, SUBLANES), width), np.float32)
    for h in range(nheads):
        m[h, h * group:(h + 1) * group] = 1.0
    return m


def _level_masks():
    t = np.arange(CHUNK)[:, None]
    s = np.arange(CHUNK)[None, :]
    masks = []
    m = 1
    while m < CHUNK:
        masks.append(((t // (2 * m) == s // (2 * m)) & (t % (2 * m) >= m) & (s % (2 * m) < m)).astype(np.float32))
        m *= 2
    return np.stack(masks)


def _packed_masks():
    lvl = _level_masks()
    nlev = int(np.log2(PACK))
    lvlp = np.stack([np.tile(lvl[lev][0:PACK, 0:PACK], (1, CHUNK // PACK)) for lev in range(nlev)])
    lane_block = np.arange(CHUNK)[None, :] // PACK
    pick = np.stack([np.broadcast_to(lane_block == b, (PACK, CHUNK)) for b in range(CHUNK // PACK)])
    bdiag = (np.arange(CHUNK)[:, None] // PACK == lane_block)
    return lvlp.astype(np.float32), pick.astype(np.float32), bdiag.astype(np.float32)


def _constants():
    lvlp, pick, bdiag = _packed_masks()
    ltri = np.tril(np.ones((CHUNK, CHUNK), np.float32))
    e_a = np.zeros((LANES, C_QK), np.float32)
    e_b = np.zeros((LANES, C_QK), np.float32)
    for h in range(C_HEADS):
        e_a[MISC_AC + h, h * C_DK:(h + 1) * C_DK] = 1.0
        e_b[MISC_BC + h, h * C_DK:(h + 1) * C_DK] = 1.0
    return dict(
        ltri=jnp.asarray(ltri, BF16),
        ltri_g=jnp.asarray(np.kron(np.eye(CHUNKS_PER_STEP, dtype=np.float32), ltri), BF16),
        e_a=jnp.asarray(e_a, BF16),
        e_b=jnp.asarray(e_b, BF16),
        ones_a=jnp.asarray(_block_ones(A_QK, A_DK, A_V, A_DV), BF16),
        ones_b=jnp.asarray(_block_ones(B_QKP, B_DKP, B_VP, B_DVP), BF16),
        ones_bv=jnp.asarray(_block_ones(B_VP, B_DVP, B_VP, B_DVP), BF16),
        ones_c=jnp.asarray(_block_ones(C_QK, C_DK, C_QK, C_DK), BF16),
        mask_a=jnp.asarray(_block_ones(A_V, A_DV, A_QK, A_DK), F32),
        mask_b=jnp.asarray(_block_ones(B_VP, B_DVP, B_QKP, B_DKP), F32),
        hmk=jnp.asarray(_head_select(A_HEADS, A_QK, A_DK), F32),
        hmk_b=jnp.asarray(_head_select(B_HEADS, B_QKP, B_DKP), F32),
        hmv_a=jnp.asarray(_head_select(A_HEADS, A_V, A_DV), F32),
        hmv_b=jnp.asarray(_head_select(B_HEADS, B_VP, B_DVP), F32),
        lvl=jnp.asarray(_level_masks(), BF16),
        lvlp=jnp.asarray(lvlp, BF16),
        pick=jnp.asarray(pick, BF16),
        bdiag=jnp.asarray(bdiag, BF16),
    )


def _sigmoid(x):
    return 1.0 / (1.0 + jnp.exp(-x))


def _softplus(x):
    return jnp.maximum(x, 0.0) + jnp.log1p(jnp.exp(-jnp.abs(x)))


def _split_bf16(a):
    hi = a.astype(BF16)
    lo = (a - hi.astype(F32)).astype(BF16)
    return hi, lo


def _dot(a, b):
    return jnp.dot(a, b, preferred_element_type=F32)


def _dot_nt(a, b):
    return lax.dot_general(a, b, (((1,), (1,)), ((), ())), preferred_element_type=F32)


def _dot_tn(a, b):
    return lax.dot_general(a, b, (((0,), (0,)), ((), ())), preferred_element_type=F32)


def _dot_lhs2(a, b_bf16):
    hi, lo = _split_bf16(a)
    both = _dot(jnp.concatenate([hi, lo], axis=0), b_bf16)
    return both[0:a.shape[0]] + both[a.shape[0]:]


def _dot_rhs2(a_bf16, b):
    hi, lo = _split_bf16(b)
    return _dot(a_bf16, hi) + _dot(a_bf16, lo)


def _dot3(a, b):
    ah, al = _split_bf16(a)
    bh, bl = _split_bf16(b)
    return _dot(ah, bh) + (_dot(ah, bl) + _dot(al, bh))


def _layer_norm(x):
    mu = jnp.mean(x, axis=-1, keepdims=True)
    xc = x - mu
    var = jnp.mean(xc * xc, axis=-1, keepdims=True)
    return xc * lax.rsqrt(var + LN_EPS)


def _block_decay_span(bcum):
    spans = [bcum[n:n + 1] - bcum[n + FSUB - 1:n + FSUB] for n in range(0, CHUNK, FSUB)]
    return functools.reduce(jnp.maximum, spans)


def _gla_intra_bounded(items):
    pairs = [(c, i) for c in range(len(items)) for i in range(CHUNK // FSUB)]
    qstacks, kscs = [], []
    for c, i in pairs:
        q, k, _, bcum, _, nheads, hmk = items[c]
        n, m = i * FSUB, (i + 1) * FSUB
        ri = bcum[n:n + 1]
        qt = q[n:m] * jnp.exp(bcum[n:m] - ri)
        kscs.append((k[0:m] * jnp.exp(ri - bcum[0:m])).astype(BF16))
        qstacks.append(jnp.concatenate([qt * hmk[h:h + 1] for h in range(nheads)], axis=0).astype(BF16))
        yield
    ssts = [_dot_nt(qs, ks) for qs, ks in zip(qstacks, kscs)]
    yield
    masked = []
    for (c, i), sst in zip(pairs, ssts):
        t_io = lax.broadcasted_iota(jnp.int32, sst.shape, 0) % FSUB + i * FSUB
        s_io = lax.broadcasted_iota(jnp.int32, sst.shape, 1)
        masked.append(jnp.where(t_io >= s_io, sst, 0.0).astype(BF16))
    rs = [_dot(ms, items[c][2][0:(i + 1) * FSUB]) for (c, i), ms in zip(pairs, masked)]
    yield
    blocks = []
    for (c, i), r in zip(pairs, rs):
        hmv, nheads = items[c][4], items[c][5]
        o_i = r[0:FSUB] * hmv[0:1]
        for h in range(1, nheads):
            o_i = o_i + r[h * FSUB:(h + 1) * FSUB] * hmv[h:h + 1]
        blocks.append(o_i)
        yield
    per = CHUNK // FSUB
    return [jnp.concatenate(blocks[c * per:(c + 1) * per], axis=0) for c in range(len(items))]


def _gla_intra_direct(q, k, v, v16, bcum, ones_kv, hmk, hmv, nheads):
    tio = lax.broadcasted_iota(jnp.int32, (SUB, q.shape[1]), 0)
    outs = []
    for i in range(NSUB):
        n = i * SUB
        qi = q[n:n + SUB]
        ki = k[n:n + SUB]
        bi = bcum[n:n + SUB]
        vi = v[n:n + SUB]
        xs = []
        for s in range(SUB):
            rel = jnp.minimum(bi - bi[s:s + 1], 0.0)
            x = qi * ki[s:s + 1] * jnp.exp(rel)
            xs.append(jnp.where(tio >= s, x, 0.0))
        x_all = jnp.concatenate(xs, axis=0).astype(BF16)
        r_all = _dot(x_all, ones_kv)
        o_i = r_all[0:SUB] * vi[0:1]
        for s in range(1, SUB):
            o_i = o_i + r_all[s * SUB:(s + 1) * SUB] * vi[s:s + 1]
        if i > 0:
            ri = bcum[n:n + 1]
            ksc = (k[0:n] * jnp.exp(ri - bcum[0:n])).astype(BF16)
            qt = qi * jnp.exp(bi - ri)
            qstack = jnp.concatenate([qt * hmk[h:h + 1] for h in range(nheads)], axis=0).astype(BF16)
            sst = _dot_nt(qstack, ksc)
            r_od = _dot(sst.astype(BF16), v16[0:n])
            for h in range(nheads):
                o_i = o_i + r_od[h * SUB:(h + 1) * SUB] * hmv[h:h + 1]
        outs.append(o_i)
    return jnp.concatenate(outs, axis=0)


def _gla_streams(streams):
    flat = [(si, c) for si, st in enumerate(streams) for c in range(len(st["chunks"]))]
    pre = []
    for si, c in flat:
        q, k, v, bcum = streams[si]["chunks"][c]
        b_last = bcum[CHUNK - 1:CHUNK, :]
        pre.append(dict(v16=v.astype(BF16), qe16=(q * jnp.exp(bcum)).astype(BF16),
                        kdec16=(k * jnp.exp(b_last - bcum)).astype(BF16), d=jnp.exp(b_last)))
        yield
    upds = [_dot_tn(p["v16"], p["kdec16"]) for p in pre]
    yield
    intra = yield from _gla_intra_bounded(
        [streams[si]["chunks"][c][0:2] + (p["v16"], streams[si]["chunks"][c][3], streams[si]["hmv"],
                                         streams[si]["nheads"], streams[si]["hmk"])
         for (si, c), p in zip(flat, pre)])
    inters = [[] for _ in streams]
    for si, st in enumerate(streams):
        state = st["st_ref"][...]
        for c in range(len(st["chunks"])):
            j = flat.index((si, c))
            inters[si].append(_dot_nt(pre[j]["qe16"], state.astype(BF16)))
            state = state * pre[j]["d"] + upds[j] * st["st_mask"]
            yield
        st["st_ref"][...] = state
    intras = [[intra[flat.index((si, c))] for c in range(len(st["chunks"]))] for si, st in enumerate(streams)]
    return inters, intras


def _unit_lower_inverses(nmats, lvl_ref, lvlp_ref, pick_ref, bdiag_ref):
    r = lax.broadcasted_iota(jnp.int32, (CHUNK, CHUNK), 0)
    c = lax.broadcasted_iota(jnp.int32, (CHUNK, CHUNK), 1)
    eye = jnp.where(r == c, 1.0, 0.0)
    nblk = CHUNK // PACK
    bdiag = bdiag_ref[...]

    def pack(x):
        return functools.reduce(lambda a, b: a + b,
                                [x[b * PACK:(b + 1) * PACK] * pick_ref[b] for b in range(nblk)])

    def unpack(xp):
        return jnp.concatenate([xp] * nblk, axis=0) * bdiag

    negs = [(-n).astype(BF16) for n in nmats]
    fulls = [(eye - n * lvl_ref[0].astype(F32)).astype(BF16) for n in nmats]
    negps = [pack(n16) for n16 in negs]
    tps = [pack(t) for t in fulls]
    for lev in range(1, lvlp_ref.shape[0]):
        pps = [_dot(np16, t).astype(BF16) for np16, t in zip(negps, fulls)]
        yield
        gps = [_dot(tp, unpack(pp)).astype(BF16) for tp, pp in zip(tps, pps)]
        yield
        tps = [tp + gp * lvlp_ref[lev] for tp, gp in zip(tps, gps)]
        fulls = [unpack(tp) for tp in tps]
    m = PACK
    lev = lvlp_ref.shape[0]
    while m < CHUNK:
        rows = [(lo, lo + m) for lo in range(m, CHUNK, 2 * m)]
        keep = [(lo - m, lo) for lo in range(m, CHUNK, 2 * m)]

        def take(x):
            return jnp.concatenate([x[a:b] for a, b in rows], axis=0)

        mask_r = take(lvl_ref[lev])
        zeros = jnp.zeros((m, CHUNK), BF16)
        prs = [_dot(take(n16), t).astype(BF16) for n16, t in zip(negs, fulls)]
        yield
        pembs = [jnp.concatenate([piece for i in range(len(rows)) for piece in (zeros, pr[i * m:(i + 1) * m])],
                                 axis=0) for pr in prs]
        grs = [_dot(take(t), pe).astype(BF16) for t, pe in zip(fulls, pembs)]
        yield
        news = [take(t) + gr * mask_r for t, gr in zip(fulls, grs)]
        fulls = [jnp.concatenate([piece for i, (a, b) in enumerate(keep)
                                  for piece in (t[a:b], new[i * m:(i + 1) * m])], axis=0)
                 for t, new in zip(fulls, news)]
        m *= 2
        lev += 1
    return fulls


PREP_DONE = "prep done"


def _interleave(*gens_and_steps):
    live = list(gens_and_steps)
    while live:
        for entry in list(live):
            gen, steps = entry
            try:
                for _ in range(steps):
                    next(gen)
            except StopIteration:
                live.remove(entry)
        yield


def _layer_kernel(x_ref, mod_ref, win_ref, wout_ref, lng_ref, lnb_ref, lb_ref, wgkh_ref, wgkl_ref, bgk_ref,
                  convw_ref, avec_ref, dtvec_ref, gain_ref,
                  ltri_ref, ltrig_ref, ea_ref, eb_ref, ones_a_ref, ones_b_ref, ones_bv_ref, ones_c_ref,
                  mask_a_ref, mask_b_ref, hmk_ref, hmkb_ref, hmva_ref, hmvb_ref, lvl_ref, lvlp_ref, pick_ref, bdiag_ref,
                  out_ref,
                  proj_ref, y_ref, ointer_ref, sa_ref, sb_ref, sc_ref, conv_ref):
    tile = x_ref.shape[1]

    @pl.when(pl.program_id(1) == 0)
    def _():
        sa_ref[...] = jnp.zeros_like(sa_ref)
        sb_ref[...] = jnp.zeros_like(sb_ref)
        sc_ref[...] = jnp.zeros_like(sc_ref)
        conv_ref[...] = jnp.zeros_like(conv_ref)

    x = x_ref[0]
    shift = mod_ref[0, 0:1, :]
    scale = mod_ref[0, 1:2, :]
    gate = mod_ref[0, 2:3, :]
    h = _layer_norm(x) * (1.0 + scale) + shift
    proj_ref[...] = _dot(h.astype(BF16), win_ref[...])

    ltri = ltri_ref[...]
    hmk = hmk_ref[...]
    hmkb = hmkb_ref[...]
    hmva = hmva_ref[...]
    hmvb = hmvb_ref[...]
    t_io = lax.broadcasted_iota(jnp.int32, (CHUNK, CHUNK), 0)
    s_io = lax.broadcasted_iota(jnp.int32, (CHUNK, CHUNK), 1)
    causal = t_io >= s_io
    strict = t_io > s_io

    def norm_gate(o, z, ones, dv, gain):
        ms = _dot((o * o).astype(BF16), ones) * (1.0 / dv)
        return (o * lax.rsqrt(ms + NORM_EPS) * gain * (z * _sigmoid(z))).astype(BF16)

    fallbacks = []

    def group_gen(g):
        chunk_rows = [pl.ds((g * CHUNKS_PER_STEP + i) * CHUNK, CHUNK) for i in range(CHUNKS_PER_STEP)]
        group_rows = pl.ds(g * (CHUNKS_PER_STEP * CHUNK), CHUNKS_PER_STEP * CHUNK)

        def chunks_of(arr):
            return [arr[i * CHUNK:(i + 1) * CHUNK] for i in range(CHUNKS_PER_STEP)]

        def ab_inputs(rows, ltri_rows):
            qa = proj_ref[rows, OFF_QA:OFF_QA + A_QK]
            fa = proj_ref[rows, OFF_FA:OFF_FA + A_QK]
            lbv = lb_ref[...]
            sg = _sigmoid(fa)
            f_a = lbv + (1.0 - lbv) * sg
            g_a = jnp.log(jnp.maximum(f_a, FORGET_FLOOR))
            k_a = (1.0 - lbv) * (1.0 - sg)
            q_a = qa * _sigmoid(qa)
            misc = proj_ref[rows, OFF_MISC:OFF_MISC + LANES]
            m_hi, m_lo = _split_bf16(misc)
            n = misc.shape[0]
            gk_hi = _dot(jnp.concatenate([m_hi, m_lo], axis=0), wgkh_ref[...])
            gk_lin = gk_hi[0:n] + (gk_hi[n:] + _dot(m_hi, wgkl_ref[...])) + bgk_ref[...]
            g_b = -_softplus(-gk_lin) * (1.0 / GLA_GATE_TEMP)
            in_a = (q_a, k_a, proj_ref[rows, OFF_IA:OFF_IA + A_V], _dot_rhs2(ltri_rows, g_a))
            in_b = (proj_ref[rows, OFF_QB:OFF_QB + B_QKP] * (B_DK ** -0.5), proj_ref[rows, OFF_KB:OFF_KB + B_QKP],
                    proj_ref[rows, OFF_VB:OFF_VB + B_VP], _dot_rhs2(ltri_rows, g_b))
            return in_a, in_b, misc

        def store_y_ab(rows, o_a, o_b):
            za = proj_ref[rows, OFF_ZA:OFF_ZA + A_V]
            y_ref[rows, OFF_YA:OFF_YA + A_V] = norm_gate(o_a, za, ones_a_ref[...], A_DV,
                                                         gain_ref[:, OFF_YA:OFF_YA + A_V])
            zb = proj_ref[rows, OFF_ZB:OFF_ZB + B_VP]
            y_ref[rows, OFF_YB:OFF_YB + B_VP] = norm_gate(o_b, zb, ones_bv_ref[...], B_DV,
                                                          gain_ref[:, OFF_YB:OFF_YB + B_VP])

        in_a, in_b, misc_g = ab_inputs(group_rows, ltrig_ref[...])
        chunks_a = list(zip(*[chunks_of(t) for t in in_a]))
        chunks_b = list(zip(*[chunks_of(t) for t in in_b]))

        def ab_work():
            stream_a = dict(chunks=chunks_a, st_ref=sa_ref, st_mask=mask_a_ref[...], hmk=hmk, hmv=hmva,
                            nheads=A_HEADS)
            stream_b = dict(chunks=chunks_b, st_ref=sb_ref, st_mask=mask_b_ref[...], hmk=hmkb, hmv=hmvb,
                            nheads=B_HEADS)
            (inter_a, inter_b), (intra_a, intra_b) = yield from _gla_streams([stream_a, stream_b])
            inter_a, inter_b = jnp.concatenate(inter_a, axis=0), jnp.concatenate(inter_b, axis=0)
            ointer_ref[group_rows, 0:A_V] = inter_a
            ointer_ref[group_rows, A_V:A_V + B_VP] = inter_b
            yield
            store_y_ab(group_rows, inter_a + jnp.concatenate(intra_a, axis=0),
                       inter_b + jnp.concatenate(intra_b, axis=0))
            yield

        span_a = jnp.max(functools.reduce(jnp.maximum, [_block_decay_span(c[3]) for c in chunks_a]))
        span_b = jnp.max(functools.reduce(jnp.maximum, [_block_decay_span(c[3]) for c in chunks_b]))
        unbounded = jnp.logical_not(jnp.maximum(span_a, span_b) <= DECAY_CAP)
        yield

        ones_c = ones_c_ref[...]
        cw = convw_ref[...]
        glen = CHUNKS_PER_STEP * CHUNK
        raw = proj_ref[group_rows, OFF_QKVC:OFF_QKVC + C_QKV]
        win = jnp.concatenate([conv_ref[...], raw], axis=0)
        conv_ref[...] = raw[glen - SUBLANES:glen]
        back1 = pltpu.roll(win, 1, axis=0)
        older = pltpu.roll(win * cw[1:2] + back1 * cw[0:1], 2, axis=0)
        acc = (win * cw[3:4] + back1 * cw[2:3] + older)[SUBLANES:SUBLANES + glen]
        yield
        qkv = acc * _sigmoid(acc)
        q_c = qkv[:, 0:C_QK]
        k_c = qkv[:, C_QK:2 * C_QK]
        v_c = qkv[:, 2 * C_QK:C_QKV]
        yield
        ssq = _dot(jnp.concatenate([q_c * q_c, k_c * k_c], axis=0).astype(BF16), ones_c)
        q_c = q_c * lax.rsqrt(ssq[0:glen] + NORM_EPS) * (C_DK ** -0.5)
        k_c = k_c * lax.rsqrt(ssq[glen:] + NORM_EPS)
        yield
        log_a = -jnp.exp(avec_ref[...]) * _softplus(misc_g + dtvec_ref[...])
        beta = _sigmoid(misc_g)
        bc_small = _dot_rhs2(ltrig_ref[...], log_a)
        bexp = _dot_lhs2(bc_small, ea_ref[...])
        beta_x = _dot_lhs2(beta, eb_ref[...])
        e_b = jnp.exp(bexp)
        bl_rows = [bexp[(i + 1) * CHUNK - 1:(i + 1) * CHUNK, :] for i in range(CHUNKS_PER_STEP)]
        bl_g = jnp.concatenate([jnp.broadcast_to(bl, (CHUNK, C_QK)) for bl in bl_rows], axis=0)
        yield
        kb_c = k_c * beta_x
        kdl_c = k_c * jnp.exp(bl_g - bexp)
        whole = dict(bexp=bexp, q16=q_c.astype(BF16), k16=k_c.astype(BF16), kb16=kb_c.astype(BF16),
                     vb16=(v_c * beta_x).astype(BF16), kbe16=(kb_c * e_b).astype(BF16),
                     qe16=(q_c * e_b).astype(BF16), kdl16=kdl_c.astype(BF16))
        prep = [dict({name: arr[i * CHUNK:(i + 1) * CHUNK] for name, arr in whole.items()},
                     b_rows=bc_small[i * CHUNK:(i + 1) * CHUNK].T, e_last=jnp.exp(bl_rows[i]))
                for i in range(CHUNKS_PER_STEP)]
        yield

        items = [(i, hd) for i in range(CHUNKS_PER_STEP) for hd in range(C_HEADS)]

        def hsl(hd):
            return slice(hd * C_DK, (hd + 1) * C_DK)

        kks = [_dot_nt(prep[i]["kb16"][:, hsl(hd)], prep[i]["k16"][:, hsl(hd)]) for i, hd in items]
        qks = [_dot_nt(prep[i]["q16"][:, hsl(hd)], prep[i]["k16"][:, hsl(hd)]) for i, hd in items]
        yield
        decs = [jnp.exp(jnp.minimum(
            prep[i]["bexp"][:, hsl(hd)] - prep[i]["b_rows"][MISC_AC + hd:MISC_AC + hd + 1, :], 0.0))
            for i, hd in items]
        nmats = [jnp.where(strict, kk * dec, 0.0) for kk, dec in zip(kks, decs)]
        attns = [jnp.where(causal, qk * dec, 0.0).astype(BF16) for qk, dec in zip(qks, decs)]
        yield PREP_DONE
        t16s = []

        def inverse_work():
            t16s.extend((yield from _unit_lower_inverses(nmats, lvl_ref, lvlp_ref, pick_ref, bdiag_ref)))

        yield from _interleave((inverse_work(), 1), (ab_work(), 3))
        us = [_dot(t16, prep[i]["vb16"][:, hsl(hd)]) for t16, (i, hd) in zip(t16s, items)]
        ws = [_dot(t16, prep[i]["kbe16"][:, hsl(hd)]).astype(BF16) for t16, (i, hd) in zip(t16s, items)]
        yield
        u16s = [u.astype(BF16) for u in us]
        kws = [_dot_tn(prep[i]["kdl16"][:, hsl(hd)], w).astype(BF16) for w, (i, hd) in zip(ws, items)]
        kus = [_dot_tn(prep[i]["kdl16"][:, hsl(hd)], u16) for u16, (i, hd) in zip(u16s, items)]
        yield

        states = [sc_ref[hd] for hd in range(C_HEADS)]
        s16s = []
        for i in range(CHUNKS_PER_STEP):
            base = i * C_HEADS
            cur16 = [s.astype(BF16) for s in states]
            s16s.extend(cur16)
            moved = [_dot(kws[base + hd], cur16[hd]) for hd in range(C_HEADS)]
            states = [states[hd] * prep[i]["e_last"][:, hsl(hd)] - moved[hd] + kus[base + hd]
                      for hd in range(C_HEADS)]
            yield
        for hd in range(C_HEADS):
            sc_ref[hd] = states[hd]

        ws_os = [_dot(jnp.concatenate([w, prep[i]["qe16"][:, hsl(hd)]], axis=0), s16)
                 for w, s16, (i, hd) in zip(ws, s16s, items)]
        yield
        vn16s = [(u - wo[0:CHUNK]).astype(BF16) for u, wo in zip(us, ws_os)]
        o_items = [wo[CHUNK:2 * CHUNK] + _dot(attn, vn16) for wo, attn, vn16 in zip(ws_os, attns, vn16s)]
        yield
        o_c = jnp.concatenate([jnp.concatenate(o_items[i * C_HEADS:(i + 1) * C_HEADS], axis=1)
                               for i in range(CHUNKS_PER_STEP)], axis=0)
        zc = proj_ref[group_rows, OFF_ZC:OFF_ZC + C_V]
        y_ref[group_rows, OFF_YC:OFF_YC + C_V] = norm_gate(o_c, zc, ones_c, C_DV, gain_ref[:, OFF_YC:OFF_YC + C_V])

        def redo_direct():
            for rows in chunk_rows:
                (q_a, k_a, v_a, bcum_a), (q_b, k_b, v_b, bcum_b), _ = ab_inputs(rows, ltri)
                o_a = _gla_intra_direct(q_a, k_a, v_a, v_a.astype(BF16), bcum_a, ones_a_ref[...], hmk, hmva,
                                        A_HEADS)
                o_b = _gla_intra_direct(q_b, k_b, v_b, v_b.astype(BF16), bcum_b, ones_b_ref[...], hmkb, hmvb,
                                        B_HEADS)
                store_y_ab(rows, ointer_ref[rows, 0:A_V] + o_a, ointer_ref[rows, A_V:A_V + B_VP] + o_b)

        fallbacks.append((unbounded, redo_direct))

    gens = [group_gen(g) for g in range(tile // (CHUNK * CHUNKS_PER_STEP))]

    def run_to_prep_done(gen):
        for token in gen:
            if token is PREP_DONE:
                return

    run_to_prep_done(gens[0])
    for g, gen in enumerate(gens):
        ahead = gens[g + 1] if g + 1 < len(gens) else None
        for _ in gen:
            if ahead is not None and next(ahead) is PREP_DONE:
                ahead = None
        if ahead is not None:
            run_to_prep_done(ahead)
    for unbounded, redo_direct in fallbacks:
        pl.when(unbounded)(redo_direct)

    out = _dot(y_ref[...], wout_ref[...])
    res = DEEPNORM_ALPHA * x + gate * out
    out_ref[0] = _layer_norm(res) * lng_ref[...] + lnb_ref[...]


def _seq_tile(seq):
    tile = CHUNK * CHUNKS_PER_STEP
    assert seq % tile == 0
    while tile * 2 <= min(seq, 512) and seq % (tile * 2) == 0:
        tile *= 2
    return tile


def _const_spec(arr):
    nd = arr.ndim
    return pl.BlockSpec(arr.shape, lambda b, j, _nd=nd: (0,) * _nd)


def _hybrid_layer_call(x, mod, w_in_p, w_out_p, ln_g, ln_b, lbv, wgk_hi, wgk_lo, bgk, conv_w, avec, dtvec, gains,
                       consts):
    bsz, seq, d = x.shape
    assert d == D_MODEL and seq % CHUNK == 0
    tile = _seq_tile(seq)
    names = ("ltri", "ltri_g", "e_a", "e_b", "ones_a", "ones_b", "ones_bv", "ones_c", "mask_a", "mask_b",
             "hmk", "hmk_b", "hmv_a", "hmv_b", "lvl", "lvlp", "pick", "bdiag")
    small = (w_in_p, w_out_p, ln_g, ln_b, lbv, wgk_hi, wgk_lo, bgk, conv_w, avec, dtvec, gains) + tuple(
        consts[n] for n in names)
    in_specs = [
        pl.BlockSpec((1, tile, D_MODEL), lambda b, j: (b, j, 0)),
        pl.BlockSpec((1, 3, D_MODEL), lambda b, j: (b, 0, 0)),
    ] + [_const_spec(a) for a in small]
    return pl.pallas_call(
        _layer_kernel,
        grid=(bsz, seq // tile),
        in_specs=in_specs,
        out_specs=pl.BlockSpec((1, tile, D_MODEL), lambda b, j: (b, j, 0)),
        out_shape=jax.ShapeDtypeStruct((bsz, seq, D_MODEL), F32),
        scratch_shapes=[
            pltpu.VMEM((tile, D_INP), F32),
            pltpu.VMEM((tile, D_MIXP), BF16),
            pltpu.VMEM((tile, A_V + B_VP), F32),
            pltpu.VMEM((A_V, A_QK), F32),
            pltpu.VMEM((B_VP, B_QKP), F32),
            pltpu.VMEM((C_HEADS, C_DK, C_DV), F32),
            pltpu.VMEM((SUBLANES, C_QKV), F32),
        ],
        compiler_params=pltpu.CompilerParams(
            dimension_semantics=("arbitrary", "arbitrary"),
            vmem_limit_bytes=VMEM_LIMIT_BYTES),
        name="hybrid_layer",
    )(x, mod, *small)


def _mod_kernel(c_ref, w_ref, b_ref, o_ref):
    c = c_ref[...]
    c_act = c * _sigmoid(c)
    o_ref[0] = _dot3(c_act, w_ref[0]) + b_ref[0]


def _ada_mod_call(c, ada_w, ada_b):
    depth, d, d3 = ada_w.shape
    bsz = c.shape[0]
    ncol = 512
    assert d3 % ncol == 0
    return pl.pallas_call(
        _mod_kernel,
        grid=(depth, d3 // ncol),
        in_specs=[
            pl.BlockSpec((bsz, d), lambda l, n: (0, 0)),
            pl.BlockSpec((1, d, ncol), lambda l, n: (l, 0, n)),
            pl.BlockSpec((1, 1, ncol), lambda l, n: (l, 0, n)),
        ],
        out_specs=pl.BlockSpec((1, bsz, ncol), lambda l, n: (l, 0, n)),
        out_shape=jax.ShapeDtypeStruct((depth, bsz, d3), F32),
        compiler_params=pltpu.CompilerParams(dimension_semantics=("arbitrary", "arbitrary")),
        name="ada_mod",
    )(c, ada_w, ada_b.reshape(depth, 1, d3))


def _lb_kernel(logit_ref, o_ref):
    depth = logit_ref.shape[0]
    rows = [logit_ref[l:l + 1, :] for l in range(depth)]
    mx = functools.reduce(jnp.maximum, rows)
    ex = [jnp.exp(r - mx) for r in rows]
    inv = 1.0 / functools.reduce(lambda a, b: a + b, ex)
    p = [e * inv for e in ex]
    run = jnp.zeros_like(p[0])
    for l in range(depth):
        run = run + p[l]
        o_ref[l:l + 1, :] = run - p[0]


def _lb_table_call(logits):
    return pl.pallas_call(
        _lb_kernel,
        out_shape=jax.ShapeDtypeStruct(logits.shape, F32),
        name="hgrn_lb",
    )(logits)


def kernel(x, c, w_in, w_out, ada_w, ada_b, ln_g, ln_b, hgrn_lb_logits, gla_w_gk, gla_b_gk, gdn_conv_w, gdn_a_log,
           gdn_dt_bias, gain_a, gain_b, gain_c):
    depth = w_in.shape[0]
    bsz = x.shape[0]
    consts = _constants()
    col_src = _padded_column_sources()
    mix_src = _padded_mix_sources()
    gk_src = np.full((B_QKP,), -1, np.int64)
    for h in range(B_HEADS):
        gk_src[h * B_DKP:h * B_DKP + B_DK] = np.arange(h * B_DK, (h + 1) * B_DK)

    mod_all = _ada_mod_call(c, ada_w, ada_b).reshape(depth, bsz, 3, D_MODEL)
    lb_table = _lb_table_call(hgrn_lb_logits.astype(F32))

    for l in range(depth):
        w_in_p = _gather_padded(w_in[l], col_src, 1).astype(BF16)
        w_out_p = _gather_padded(w_out[l], mix_src, 0).astype(BF16)
        wgk = _gather_padded(gla_w_gk[l].astype(F32), gk_src, 1)
        wgk = jnp.zeros((LANES, B_QKP), F32).at[MISC_LR:MISC_LR + GLA_RANK].set(wgk)
        wgk_hi = wgk.astype(BF16)
        wgk_lo = (wgk - wgk_hi.astype(F32)).astype(BF16)
        bgk = _gather_padded(gla_b_gk[l].astype(F32), gk_src, 0).reshape(1, B_QKP)
        avec = jnp.zeros((1, LANES), F32).at[0, MISC_AC:MISC_AC + C_HEADS].set(gdn_a_log[l].astype(F32))
        dtvec = jnp.zeros((1, LANES), F32).at[0, MISC_AC:MISC_AC + C_HEADS].set(gdn_dt_bias[l].astype(F32))
        gain_b_p = jnp.zeros((B_DVP,), F32).at[:B_DV].set(gain_b[l].astype(F32))
        gains = jnp.concatenate([jnp.tile(gain_a[l].astype(F32), A_HEADS), jnp.tile(gain_b_p, B_HEADS),
                                 jnp.tile(gain_c[l].astype(F32), C_HEADS)]).reshape(1, D_MIXP)
        x = _hybrid_layer_call(
            x, mod_all[l], w_in_p, w_out_p, ln_g[l].reshape(1, D_MODEL), ln_b[l].reshape(1, D_MODEL),
            lb_table[l].reshape(1, A_QK), wgk_hi, wgk_lo, bgk, gdn_conv_w[l].astype(F32), avec, dtvec, gains, consts)
    return x
```

```python
import functools

import numpy as np
import jax
import jax.numpy as jnp
from jax import lax
from jax.experimental import pallas as pl
from jax.experimental.pallas import tpu as pltpu

F32 = jnp.float32
BF16 = jnp.bfloat16

D_MODEL = 1024
DEPTH = 2
CHUNK = 64
CONV_WIDTH = 4
assert CONV_WIDTH == 4
A_HEADS, A_DK, A_DV = 4, 64, 64
B_HEADS, B_DK, B_DV = 4, 48, 96
GLA_RANK = 16
GLA_GATE_TEMP = 16.0
C_HEADS, C_DK, C_DV = 6, 64, 64
DEEPNORM_ALPHA = (2 * DEPTH) ** 0.25
LN_EPS = 1e-5
NORM_EPS = 1e-6
FORGET_FLOOR = 1e-30

A_QK = A_HEADS * A_DK
A_V = A_HEADS * A_DV
B_QK = B_HEADS * B_DK
B_V = B_HEADS * B_DV
C_QK = C_HEADS * C_DK
C_V = C_HEADS * C_DV
C_QKV = 2 * C_QK + C_V
SPLIT_SIZES = (A_QK, A_QK, A_V, A_V, B_QK, B_QK, B_V, GLA_RANK, B_V, C_QKV, C_HEADS, C_HEADS, C_V)
D_IN = sum(SPLIT_SIZES)

LANES = 128
SUBLANES = 8
VMEM_LIMIT_BYTES = 56 * 1024 * 1024

B_DKP = B_DK
B_DVP = B_DV
B_QKP = B_HEADS * B_DKP
B_VP = B_HEADS * B_DVP
B_QK_SLOT = 256
SUB = 16
NSUB = CHUNK // SUB
PACK = 16
FSUB = 32
DECAY_CAP = 60.0
CHUNKS_PER_STEP = 4

OFF_QA = 0
OFF_FA = OFF_QA + A_QK
OFF_IA = OFF_FA + A_QK
OFF_ZA = OFF_IA + A_V
OFF_QB = OFF_ZA + A_V
OFF_KB = OFF_QB + B_QK_SLOT
OFF_VB = OFF_KB + B_QK_SLOT
OFF_ZB = OFF_VB + B_VP
OFF_QKVC = OFF_ZB + B_VP
OFF_ZC = OFF_QKVC + C_QKV
D_INP = OFF_ZC + C_V
OFF_MISC = OFF_QB + B_QK_SLOT - LANES
MISC_LR = B_QKP - (B_QK_SLOT - LANES)
MISC_AC = MISC_LR + GLA_RANK
MISC_BC = MISC_AC + C_HEADS
assert MISC_BC + C_HEADS <= LANES and D_INP % LANES == 0
D_MIXP = A_V + B_VP + C_V
OFF_YA, OFF_YB, OFF_YC = 0, A_V, A_V + B_VP


def _padded_column_sources():
    offs = np.concatenate([[0], np.cumsum(SPLIT_SIZES)])
    (o_qa, o_fa, o_ia, o_za, o_qb, o_kb, o_vb, o_lr, o_zb, o_qkvc, o_ac, o_bc, o_zc) = offs[:-1]
    src = np.full((D_INP,), -1, np.int64)
    for dst, s0 in ((OFF_QA, o_qa), (OFF_FA, o_fa), (OFF_IA, o_ia), (OFF_ZA, o_za)):
        src[dst:dst + A_QK] = np.arange(s0, s0 + A_QK)
    for h in range(B_HEADS):
        src[OFF_QB + h * B_DKP:OFF_QB + h * B_DKP + B_DK] = np.arange(o_qb + h * B_DK, o_qb + (h + 1) * B_DK)
        src[OFF_KB + h * B_DKP:OFF_KB + h * B_DKP + B_DK] = np.arange(o_kb + h * B_DK, o_kb + (h + 1) * B_DK)
        src[OFF_VB + h * B_DVP:OFF_VB + h * B_DVP + B_DV] = np.arange(o_vb + h * B_DV, o_vb + (h + 1) * B_DV)
        src[OFF_ZB + h * B_DVP:OFF_ZB + h * B_DVP + B_DV] = np.arange(o_zb + h * B_DV, o_zb + (h + 1) * B_DV)
    src[OFF_QKVC:OFF_QKVC + C_QKV] = np.arange(o_qkvc, o_qkvc + C_QKV)
    src[OFF_ZC:OFF_ZC + C_V] = np.arange(o_zc, o_zc + C_V)
    src[OFF_MISC + MISC_LR:OFF_MISC + MISC_LR + GLA_RANK] = np.arange(o_lr, o_lr + GLA_RANK)
    src[OFF_MISC + MISC_AC:OFF_MISC + MISC_AC + C_HEADS] = np.arange(o_ac, o_ac + C_HEADS)
    src[OFF_MISC + MISC_BC:OFF_MISC + MISC_BC + C_HEADS] = np.arange(o_bc, o_bc + C_HEADS)
    return src


def _padded_mix_sources():
    src = np.full((D_MIXP,), -1, np.int64)
    src[OFF_YA:OFF_YA + A_V] = np.arange(0, A_V)
    for h in range(B_HEADS):
        src[OFF_YB + h * B_DVP:OFF_YB + h * B_DVP + B_DV] = np.arange(A_V + h * B_DV, A_V + (h + 1) * B_DV)
    src[OFF_YC:OFF_YC + C_V] = np.arange(A_V + B_V, A_V + B_V + C_V)
    return src


def _gather_padded(w, src, axis):
    idx = jnp.asarray(np.maximum(src, 0), jnp.int32)
    valid = jnp.asarray(src >= 0)
    shape = [1] * w.ndim
    shape[axis] = src.shape[0]
    return jnp.where(valid.reshape(shape), jnp.take(w, idx, axis=axis), 0.0)


def _block_ones(rows, rgroup, cols, cgroup):
    r = np.arange(rows)[:, None] // rgroup
    c = np.arange(cols)[None, :] // cgroup
    return (r == c).astype(np.float32)


def _head_select(nheads, width, group):
    m = np.zeros((max(nheads, SUBLANES), width), np.float32)
    for h in range(nheads):
        m[h, h * group:(h + 1) * group] = 1.0
    return m


def _level_masks():
    t = np.arange(CHUNK)[:, None]
    s = np.arange(CHUNK)[None, :]
    masks = []
    m = 1
    while m < CHUNK:
        masks.append(((t // (2 * m) == s // (2 * m)) & (t % (2 * m) >= m) & (s % (2 * m) < m)).astype(np.float32))
        m *= 2
    return np.stack(masks)


def _packed_masks():
    lvl = _level_masks()
    nlev = int(np.log2(PACK))
    lvlp = np.stack([np.tile(lvl[lev][0:PACK, 0:PACK], (1, CHUNK // PACK)) for lev in range(nlev)])
    lane_block = np.arange(CHUNK)[None, :] // PACK
    pick = np.stack([np.broadcast_to(lane_block == b, (PACK, CHUNK)) for b in range(CHUNK // PACK)])
    bdiag = (np.arange(CHUNK)[:, None] // PACK == lane_block)
    return lvlp.astype(np.float32), pick.astype(np.float32), bdiag.astype(np.float32)


def _constants():
    lvlp, pick, bdiag = _packed_masks()
    ltri = np.tril(np.ones((CHUNK, CHUNK), np.float32))
    e_a = np.zeros((LANES, C_QK), np.float32)
    e_b = np.zeros((LANES, C_QK), np.float32)
    for h in range(C_HEADS):
        e_a[MISC_AC + h, h * C_DK:(h + 1) * C_DK] = 1.0
        e_b[MISC_BC + h, h * C_DK:(h + 1) * C_DK] = 1.0
    return dict(
        ltri=jnp.asarray(ltri, BF16),
        ltri_g=jnp.asarray(np.kron(np.eye(CHUNKS_PER_STEP, dtype=np.float32), ltri), BF16),
        e_a=jnp.asarray(e_a, BF16),
        e_b=jnp.asarray(e_b, BF16),
        ones_a=jnp.asarray(_block_ones(A_QK, A_DK, A_V, A_DV), BF16),
        ones_b=jnp.asarray(_block_ones(B_QKP, B_DKP, B_VP, B_DVP), BF16),
        ones_bv=jnp.asarray(_block_ones(B_VP, B_DVP, B_VP, B_DVP), BF16),
        ones_c=jnp.asarray(_block_ones(C_QK, C_DK, C_QK, C_DK), BF16),
        mask_a=jnp.asarray(_block_ones(A_V, A_DV, A_QK, A_DK), F32),
        mask_b=jnp.asarray(_block_ones(B_VP, B_DVP, B_QKP, B_DKP), F32),
        hmk=jnp.asarray(_head_select(A_HEADS, A_QK, A_DK), F32),
        hmk_b=jnp.asarray(_head_select(B_HEADS, B_QKP, B_DKP), F32),
        hmv_a=jnp.asarray(_head_select(A_HEADS, A_V, A_DV), F32),
        hmv_b=jnp.asarray(_head_select(B_HEADS, B_VP, B_DVP), F32),
        lvl=jnp.asarray(_level_masks(), BF16),
        lvlp=jnp.asarray(lvlp, BF16),
        pick=jnp.asarray(pick, BF16),
        bdiag=jnp.asarray(bdiag, BF16),
    )


def _sigmoid(x):
    return 1.0 / (1.0 + jnp.exp(-x))


def _softplus(x):
    return jnp.maximum(x, 0.0) + jnp.log1p(jnp.exp(-jnp.abs(x)))


def _split_bf16(a):
    hi = a.astype(BF16)
    lo = (a - hi.astype(F32)).astype(BF16)
    return hi, lo


def _dot(a, b):
    return jnp.dot(a, b, preferred_element_type=F32)


def _dot_nt(a, b):
    return lax.dot_general(a, b, (((1,), (1,)), ((), ())), preferred_element_type=F32)


def _dot_tn(a, b):
    return lax.dot_general(a, b, (((0,), (0,)), ((), ())), preferred_element_type=F32)


def _dot_lhs2(a, b_bf16):
    hi, lo = _split_bf16(a)
    both = _dot(jnp.concatenate([hi, lo], axis=0), b_bf16)
    return both[0:a.shape[0]] + both[a.shape[0]:]


def _dot_rhs2(a_bf16, b):
    hi, lo = _split_bf16(b)
    return _dot(a_bf16, hi) + _dot(a_bf16, lo)


def _dot3(a, b):
    ah, al = _split_bf16(a)
    bh, bl = _split_bf16(b)
    return _dot(ah, bh) + (_dot(ah, bl) + _dot(al, bh))


def _layer_norm(x):
    mu = jnp.mean(x, axis=-1, keepdims=True)
    xc = x - mu
    var = jnp.mean(xc * xc, axis=-1, keepdims=True)
    return xc * lax.rsqrt(var + LN_EPS)


def _block_decay_span(bcum):
    spans = [bcum[n:n + 1] - bcum[n + FSUB - 1:n + FSUB] for n in range(0, CHUNK, FSUB)]
    return functools.reduce(jnp.maximum, spans)


def _gla_intra_bounded(items):
    pairs = [(c, i) for c in range(len(items)) for i in range(CHUNK // FSUB)]
    qstacks, kscs = [], []
    for c, i in pairs:
        q, k, _, bcum, _, nheads, hmk = items[c]
        n, m = i * FSUB, (i + 1) * FSUB
        ri = bcum[n:n + 1]
        qt = q[n:m] * jnp.exp(bcum[n:m] - ri)
        kscs.append((k[0:m] * jnp.exp(ri - bcum[0:m])).astype(BF16))
        qt16 = qt.astype(BF16)
        hmk16 = hmk.astype(BF16)
        qstacks.append(jnp.concatenate([qt16 * hmk16[h:h + 1] for h in range(nheads)], axis=0))
        yield
    ssts = [_dot_nt(qs, ks) for qs, ks in zip(qstacks, kscs)]
    yield
    masked = []
    for (c, i), sst in zip(pairs, ssts):
        t_io = lax.broadcasted_iota(jnp.int32, sst.shape, 0) % FSUB + i * FSUB
        s_io = lax.broadcasted_iota(jnp.int32, sst.shape, 1)
        masked.append(jnp.where(t_io >= s_io, sst, 0.0).astype(BF16))
    rs = [_dot(ms, items[c][2][0:(i + 1) * FSUB]) for (c, i), ms in zip(pairs, masked)]
    yield
    blocks = []
    for (c, i), r in zip(pairs, rs):
        hmv, nheads = items[c][4], items[c][5]
        o_i = r[0:FSUB] * hmv[0:1]
        for h in range(1, nheads):
            o_i = o_i + r[h * FSUB:(h + 1) * FSUB] * hmv[h:h + 1]
        blocks.append(o_i)
        yield
    per = CHUNK // FSUB
    return [jnp.concatenate(blocks[c * per:(c + 1) * per], axis=0) for c in range(len(items))]


def _gla_intra_direct(q, k, v, v16, bcum, ones_kv, hmk, hmv, nheads):
    tio = lax.broadcasted_iota(jnp.int32, (SUB, q.shape[1]), 0)
    outs = []
    for i in range(NSUB):
        n = i * SUB
        qi = q[n:n + SUB]
        ki = k[n:n + SUB]
        bi = bcum[n:n + SUB]
        vi = v[n:n + SUB]
        xs = []
        for s in range(SUB):
            rel = jnp.minimum(bi - bi[s:s + 1], 0.0)
            x = qi * ki[s:s + 1] * jnp.exp(rel)
            xs.append(jnp.where(tio >= s, x, 0.0))
        x_all = jnp.concatenate(xs, axis=0).astype(BF16)
        r_all = _dot(x_all, ones_kv)
        o_i = r_all[0:SUB] * vi[0:1]
        for s in range(1, SUB):
            o_i = o_i + r_all[s * SUB:(s + 1) * SUB] * vi[s:s + 1]
        if i > 0:
            ri = bcum[n:n + 1]
            ksc = (k[0:n] * jnp.exp(ri - bcum[0:n])).astype(BF16)
            qt = qi * jnp.exp(bi - ri)
            qstack = jnp.concatenate([qt * hmk[h:h + 1] for h in range(nheads)], axis=0).astype(BF16)
            sst = _dot_nt(qstack, ksc)
            r_od = _dot(sst.astype(BF16), v16[0:n])
            for h in range(nheads):
                o_i = o_i + r_od[h * SUB:(h + 1) * SUB] * hmv[h:h + 1]
        outs.append(o_i)
    return jnp.concatenate(outs, axis=0)


def _gla_streams(streams):
    flat = [(si, c) for si, st in enumerate(streams) for c in range(len(st["chunks"]))]
    pre = []
    for si, c in flat:
        q, k, v, bcum = streams[si]["chunks"][c]
        b_last = bcum[CHUNK - 1:CHUNK, :]
        pre.append(dict(v16=v.astype(BF16), qe16=(q * jnp.exp(bcum)).astype(BF16),
                        kdec16=(k * jnp.exp(b_last - bcum)).astype(BF16), d=jnp.exp(b_last)))
        yield
    upds = [_dot_tn(p["v16"], p["kdec16"]) for p in pre]
    yield
    intra = yield from _gla_intra_bounded(
        [streams[si]["chunks"][c][0:2] + (p["v16"], streams[si]["chunks"][c][3], streams[si]["hmv"],
                                         streams[si]["nheads"], streams[si]["hmk"])
         for (si, c), p in zip(flat, pre)])
    inters = [[] for _ in streams]
    for si, st in enumerate(streams):
        state = st["st_ref"][...]
        for c in range(len(st["chunks"])):
            j = flat.index((si, c))
            inters[si].append(_dot_nt(pre[j]["qe16"], state.astype(BF16)))
            state = state * pre[j]["d"] + upds[j] * st["st_mask"]
            yield
        st["st_ref"][...] = state
    intras = [[intra[flat.index((si, c))] for c in range(len(st["chunks"]))] for si, st in enumerate(streams)]
    return inters, intras


def _unit_lower_inverses(nmats, lvl_ref, lvlp_ref, pick_ref, bdiag_ref):
    r = lax.broadcasted_iota(jnp.int32, (CHUNK, CHUNK), 0)
    c = lax.broadcasted_iota(jnp.int32, (CHUNK, CHUNK), 1)
    eye = jnp.where(r == c, 1.0, 0.0)
    nblk = CHUNK // PACK
    bdiag = bdiag_ref[...]

    def pack(x):
        return functools.reduce(lambda a, b: a + b,
                                [x[b * PACK:(b + 1) * PACK] * pick_ref[b] for b in range(nblk)])

    def unpack(xp):
        return jnp.concatenate([xp] * nblk, axis=0) * bdiag

    negs = [(-n).astype(BF16) for n in nmats]
    fulls = [(eye - n * lvl_ref[0].astype(F32)).astype(BF16) for n in nmats]
    negps = [pack(n16) for n16 in negs]
    tps = [pack(t) for t in fulls]
    for lev in range(1, lvlp_ref.shape[0]):
        pps = [_dot(np16, t).astype(BF16) for np16, t in zip(negps, fulls)]
        yield
        gps = [_dot(tp, unpack(pp)).astype(BF16) for tp, pp in zip(tps, pps)]
        yield
        tps = [tp + gp * lvlp_ref[lev] for tp, gp in zip(tps, gps)]
        fulls = [unpack(tp) for tp in tps]
    m = PACK
    lev = lvlp_ref.shape[0]
    while m < CHUNK:
        rows = [(lo, lo + m) for lo in range(m, CHUNK, 2 * m)]
        keep = [(lo - m, lo) for lo in range(m, CHUNK, 2 * m)]

        def take(x):
            return jnp.concatenate([x[a:b] for a, b in rows], axis=0)

        mask_r = take(lvl_ref[lev])
        zeros = jnp.zeros((m, CHUNK), BF16)
        prs = [_dot(take(n16), t).astype(BF16) for n16, t in zip(negs, fulls)]
        yield
        pembs = [jnp.concatenate([piece for i in range(len(rows)) for piece in (zeros, pr[i * m:(i + 1) * m])],
                                 axis=0) for pr in prs]
        grs = [_dot(take(t), pe).astype(BF16) for t, pe in zip(fulls, pembs)]
        yield
        news = [take(t) + gr * mask_r for t, gr in zip(fulls, grs)]
        fulls = [jnp.concatenate([piece for i, (a, b) in enumerate(keep)
                                  for piece in (t[a:b], new[i * m:(i + 1) * m])], axis=0)
                 for t, new in zip(fulls, news)]
        m *= 2
        lev += 1
    return fulls


PREP_DONE = "prep done"


def _interleave(*gens_and_steps):
    live = list(gens_and_steps)
    while live:
        for entry in list(live):
            gen, steps = entry
            try:
                for _ in range(steps):
                    next(gen)
            except StopIteration:
                live.remove(entry)
        yield


def _layer_kernel(x_ref, mod_ref, win_ref, wout_ref, lng_ref, lnb_ref, lb_ref, wgkh_ref, wgkl_ref, bgk_ref,
                  convw_ref, avec_ref, dtvec_ref, gain_ref,
                  ltri_ref, ltrig_ref, ea_ref, eb_ref, ones_a_ref, ones_b_ref, ones_bv_ref, ones_c_ref,
                  mask_a_ref, mask_b_ref, hmk_ref, hmkb_ref, hmva_ref, hmvb_ref, lvl_ref, lvlp_ref, pick_ref, bdiag_ref,
                  out_ref,
                  proj_ref, y_ref, ointer_ref, sa_ref, sb_ref, sc_ref, conv_ref):
    tile = x_ref.shape[1]

    @pl.when(pl.program_id(1) == 0)
    def _():
        sa_ref[...] = jnp.zeros_like(sa_ref)
        sb_ref[...] = jnp.zeros_like(sb_ref)
        sc_ref[...] = jnp.zeros_like(sc_ref)
        conv_ref[...] = jnp.zeros_like(conv_ref)

    x = x_ref[0]
    shift = mod_ref[0, 0:1, :]
    scale = mod_ref[0, 1:2, :]
    gate = mod_ref[0, 2:3, :]
    h = _layer_norm(x) * (1.0 + scale) + shift
    proj_ref[...] = _dot(h.astype(BF16), win_ref[...])

    ltri = ltri_ref[...]
    hmk = hmk_ref[...]
    hmkb = hmkb_ref[...]
    hmva = hmva_ref[...]
    hmvb = hmvb_ref[...]
    t_io = lax.broadcasted_iota(jnp.int32, (CHUNK, CHUNK), 0)
    s_io = lax.broadcasted_iota(jnp.int32, (CHUNK, CHUNK), 1)
    causal = t_io >= s_io
    strict = t_io > s_io

    def norm_gate(o, z, ones, dv, gain):
        ms = _dot((o * o).astype(BF16), ones) * (1.0 / dv)
        return (o * lax.rsqrt(ms + NORM_EPS) * gain * (z * _sigmoid(z))).astype(BF16)

    fallbacks = []

    def group_gen(g):
        chunk_rows = [pl.ds((g * CHUNKS_PER_STEP + i) * CHUNK, CHUNK) for i in range(CHUNKS_PER_STEP)]
        group_rows = pl.ds(g * (CHUNKS_PER_STEP * CHUNK), CHUNKS_PER_STEP * CHUNK)

        def chunks_of(arr):
            return [arr[i * CHUNK:(i + 1) * CHUNK] for i in range(CHUNKS_PER_STEP)]

        def ab_inputs(rows, ltri_rows):
            qa = proj_ref[rows, OFF_QA:OFF_QA + A_QK]
            fa = proj_ref[rows, OFF_FA:OFF_FA + A_QK]
            lbv = lb_ref[...]
            sg = _sigmoid(fa)
            f_a = lbv + (1.0 - lbv) * sg
            g_a = jnp.log(jnp.maximum(f_a, FORGET_FLOOR))
            k_a = (1.0 - lbv) * (1.0 - sg)
            q_a = qa * _sigmoid(qa)
            misc = proj_ref[rows, OFF_MISC:OFF_MISC + LANES]
            m_hi, m_lo = _split_bf16(misc)
            n = misc.shape[0]
            gk_hi = _dot(jnp.concatenate([m_hi, m_lo], axis=0), wgkh_ref[...])
            gk_lin = gk_hi[0:n] + (gk_hi[n:] + _dot(m_hi, wgkl_ref[...])) + bgk_ref[...]
            g_b = -_softplus(-gk_lin) * (1.0 / GLA_GATE_TEMP)
            in_a = (q_a, k_a, proj_ref[rows, OFF_IA:OFF_IA + A_V], _dot_rhs2(ltri_rows, g_a))
            in_b = (proj_ref[rows, OFF_QB:OFF_QB + B_QKP] * (B_DK ** -0.5), proj_ref[rows, OFF_KB:OFF_KB + B_QKP],
                    proj_ref[rows, OFF_VB:OFF_VB + B_VP], _dot_rhs2(ltri_rows, g_b))
            return in_a, in_b, misc

        def store_y_ab(rows, o_a, o_b):
            za = proj_ref[rows, OFF_ZA:OFF_ZA + A_V]
            y_ref[rows, OFF_YA:OFF_YA + A_V] = norm_gate(o_a, za, ones_a_ref[...], A_DV,
                                                         gain_ref[:, OFF_YA:OFF_YA + A_V])
            zb = proj_ref[rows, OFF_ZB:OFF_ZB + B_VP]
            y_ref[rows, OFF_YB:OFF_YB + B_VP] = norm_gate(o_b, zb, ones_bv_ref[...], B_DV,
                                                          gain_ref[:, OFF_YB:OFF_YB + B_VP])

        in_a, in_b, misc_g = ab_inputs(group_rows, ltrig_ref[...])
        chunks_a = list(zip(*[chunks_of(t) for t in in_a]))
        chunks_b = list(zip(*[chunks_of(t) for t in in_b]))

        def ab_work():
            stream_a = dict(chunks=chunks_a, st_ref=sa_ref, st_mask=mask_a_ref[...], hmk=hmk, hmv=hmva,
                            nheads=A_HEADS)
            stream_b = dict(chunks=chunks_b, st_ref=sb_ref, st_mask=mask_b_ref[...], hmk=hmkb, hmv=hmvb,
                            nheads=B_HEADS)
            (inter_a, inter_b), (intra_a, intra_b) = yield from _gla_streams([stream_a, stream_b])
            inter_a, inter_b = jnp.concatenate(inter_a, axis=0), jnp.concatenate(inter_b, axis=0)
            ointer_ref[group_rows, 0:A_V] = inter_a
            ointer_ref[group_rows, A_V:A_V + B_VP] = inter_b
            yield
            store_y_ab(group_rows, inter_a + jnp.concatenate(intra_a, axis=0),
                       inter_b + jnp.concatenate(intra_b, axis=0))
            yield

        span_a = jnp.max(functools.reduce(jnp.maximum, [_block_decay_span(c[3]) for c in chunks_a]))
        span_b = jnp.max(functools.reduce(jnp.maximum, [_block_decay_span(c[3]) for c in chunks_b]))
        unbounded = jnp.logical_not(jnp.maximum(span_a, span_b) <= DECAY_CAP)
        yield

        ones_c = ones_c_ref[...]
        cw = convw_ref[...]
        glen = CHUNKS_PER_STEP * CHUNK
        raw = proj_ref[group_rows, OFF_QKVC:OFF_QKVC + C_QKV]
        win = jnp.concatenate([conv_ref[...], raw], axis=0)
        conv_ref[...] = raw[glen - SUBLANES:glen]
        back1 = pltpu.roll(win, 1, axis=0)
        older = pltpu.roll(win * cw[1:2] + back1 * cw[0:1], 2, axis=0)
        acc = (win * cw[3:4] + back1 * cw[2:3] + older)[SUBLANES:SUBLANES + glen]
        yield
        qkv = acc * _sigmoid(acc)
        q_c = qkv[:, 0:C_QK]
        k_c = qkv[:, C_QK:2 * C_QK]
        v_c = qkv[:, 2 * C_QK:C_QKV]
        yield
        ssq = _dot(jnp.concatenate([q_c * q_c, k_c * k_c], axis=0).astype(BF16), ones_c)
        q_c = q_c * lax.rsqrt(ssq[0:glen] + NORM_EPS) * (C_DK ** -0.5)
        k_c = k_c * lax.rsqrt(ssq[glen:] + NORM_EPS)
        yield
        log_a = -jnp.exp(avec_ref[...]) * _softplus(misc_g + dtvec_ref[...])
        beta = _sigmoid(misc_g)
        bc_small = _dot_rhs2(ltrig_ref[...], log_a)
        bexp = _dot_lhs2(bc_small, ea_ref[...])
        beta_x = _dot_lhs2(beta, eb_ref[...])
        e_b = jnp.exp(bexp)
        bl_rows = [bexp[(i + 1) * CHUNK - 1:(i + 1) * CHUNK, :] for i in range(CHUNKS_PER_STEP)]
        bl_g = jnp.concatenate([jnp.broadcast_to(bl, (CHUNK, C_QK)) for bl in bl_rows], axis=0)
        yield
        kb_c = k_c * beta_x
        kdl_c = k_c * jnp.exp(bl_g - bexp)
        whole = dict(bexp=bexp, q16=q_c.astype(BF16), k16=k_c.astype(BF16), kb16=kb_c.astype(BF16),
                     vb16=(v_c * beta_x).astype(BF16), kbe16=(kb_c * e_b).astype(BF16),
                     qe16=(q_c * e_b).astype(BF16), kdl16=kdl_c.astype(BF16))
        prep = [dict({name: arr[i * CHUNK:(i + 1) * CHUNK] for name, arr in whole.items()},
                     b_rows=bc_small[i * CHUNK:(i + 1) * CHUNK].T, e_last=jnp.exp(bl_rows[i]))
                for i in range(CHUNKS_PER_STEP)]
        yield

        items = [(i, hd) for i in range(CHUNKS_PER_STEP) for hd in range(C_HEADS)]

        def hsl(hd):
            return slice(hd * C_DK, (hd + 1) * C_DK)

        kks = [_dot_nt(prep[i]["kb16"][:, hsl(hd)], prep[i]["k16"][:, hsl(hd)]) for i, hd in items]
        qks = [_dot_nt(prep[i]["q16"][:, hsl(hd)], prep[i]["k16"][:, hsl(hd)]) for i, hd in items]
        yield
        decs = [jnp.exp(jnp.minimum(
            prep[i]["bexp"][:, hsl(hd)] - prep[i]["b_rows"][MISC_AC + hd:MISC_AC + hd + 1, :], 0.0))
            for i, hd in items]
        nmats = [jnp.where(strict, kk * dec, 0.0) for kk, dec in zip(kks, decs)]
        attns = [jnp.where(causal, qk * dec, 0.0).astype(BF16) for qk, dec in zip(qks, decs)]
        yield PREP_DONE
        t16s = []

        def inverse_work():
            t16s.extend((yield from _unit_lower_inverses(nmats, lvl_ref, lvlp_ref, pick_ref, bdiag_ref)))

        yield from _interleave((inverse_work(), 1), (ab_work(), 3))
        us = [_dot(t16, prep[i]["vb16"][:, hsl(hd)]) for t16, (i, hd) in zip(t16s, items)]
        ws = [_dot(t16, prep[i]["kbe16"][:, hsl(hd)]).astype(BF16) for t16, (i, hd) in zip(t16s, items)]
        yield
        u16s = [u.astype(BF16) for u in us]
        kws = [_dot_tn(prep[i]["kdl16"][:, hsl(hd)], w).astype(BF16) for w, (i, hd) in zip(ws, items)]
        kus = [_dot_tn(prep[i]["kdl16"][:, hsl(hd)], u16) for u16, (i, hd) in zip(u16s, items)]
        yield

        states = [sc_ref[hd] for hd in range(C_HEADS)]
        s16s = []
        for i in range(CHUNKS_PER_STEP):
            base = i * C_HEADS
            cur16 = [s.astype(BF16) for s in states]
            s16s.extend(cur16)
            moved = [_dot(kws[base + hd], cur16[hd]) for hd in range(C_HEADS)]
            states = [states[hd] * prep[i]["e_last"][:, hsl(hd)] - moved[hd] + kus[base + hd]
                      for hd in range(C_HEADS)]
            yield
        for hd in range(C_HEADS):
            sc_ref[hd] = states[hd]

        ws_os = [_dot(jnp.concatenate([w, prep[i]["qe16"][:, hsl(hd)]], axis=0), s16)
                 for w, s16, (i, hd) in zip(ws, s16s, items)]
        yield
        vn16s = [(u - wo[0:CHUNK]).astype(BF16) for u, wo in zip(us, ws_os)]
        o_items = [wo[CHUNK:2 * CHUNK] + _dot(attn, vn16) for wo, attn, vn16 in zip(ws_os, attns, vn16s)]
        yield
        o_c = jnp.concatenate([jnp.concatenate(o_items[i * C_HEADS:(i + 1) * C_HEADS], axis=1)
                               for i in range(CHUNKS_PER_STEP)], axis=0)
        zc = proj_ref[group_rows, OFF_ZC:OFF_ZC + C_V]
        y_ref[group_rows, OFF_YC:OFF_YC + C_V] = norm_gate(o_c, zc, ones_c, C_DV, gain_ref[:, OFF_YC:OFF_YC + C_V])

        def redo_direct():
            for rows in chunk_rows:
                (q_a, k_a, v_a, bcum_a), (q_b, k_b, v_b, bcum_b), _ = ab_inputs(rows, ltri)
                o_a = _gla_intra_direct(q_a, k_a, v_a, v_a.astype(BF16), bcum_a, ones_a_ref[...], hmk, hmva,
                                        A_HEADS)
                o_b = _gla_intra_direct(q_b, k_b, v_b, v_b.astype(BF16), bcum_b, ones_b_ref[...], hmkb, hmvb,
                                        B_HEADS)
                store_y_ab(rows, ointer_ref[rows, 0:A_V] + o_a, ointer_ref[rows, A_V:A_V + B_VP] + o_b)

        fallbacks.append((unbounded, redo_direct))

    gens = [group_gen(g) for g in range(tile // (CHUNK * CHUNKS_PER_STEP))]

    def run_to_prep_done(gen):
        for token in gen:
            if token is PREP_DONE:
                return

    run_to_prep_done(gens[0])
    for g, gen in enumerate(gens):
        ahead = gens[g + 1] if g + 1 < len(gens) else None
        for _ in gen:
            if ahead is not None and next(ahead) is PREP_DONE:
                ahead = None
        if ahead is not None:
            run_to_prep_done(ahead)
    for unbounded, redo_direct in fallbacks:
        pl.when(unbounded)(redo_direct)

    out = _dot(y_ref[...], wout_ref[...])
    res = DEEPNORM_ALPHA * x + gate * out
    out_ref[0] = _layer_norm(res) * lng_ref[...] + lnb_ref[...]


def _seq_tile(seq):
    tile = CHUNK * CHUNKS_PER_STEP
    assert seq % tile == 0
    while tile * 2 <= min(seq, 512) and seq % (tile * 2) == 0:
        tile *= 2
    return tile


def _const_spec(arr):
    nd = arr.ndim
    return pl.BlockSpec(arr.shape, lambda b, j, _nd=nd: (0,) * _nd)


def _hybrid_layer_call(x, mod, w_in_p, w_out_p, ln_g, ln_b, lbv, wgk_hi, wgk_lo, bgk, conv_w, avec, dtvec, gains,
                       consts):
    bsz, seq, d = x.shape
    assert d == D_MODEL and seq % CHUNK == 0
    tile = _seq_tile(seq)
    names = ("ltri", "ltri_g", "e_a", "e_b", "ones_a", "ones_b", "ones_bv", "ones_c", "mask_a", "mask_b",
             "hmk", "hmk_b", "hmv_a", "hmv_b", "lvl", "lvlp", "pick", "bdiag")
    small = (w_in_p, w_out_p, ln_g, ln_b, lbv, wgk_hi, wgk_lo, bgk, conv_w, avec, dtvec, gains) + tuple(
        consts[n] for n in names)
    in_specs = [
        pl.BlockSpec((1, tile, D_MODEL), lambda b, j: (b, j, 0)),
        pl.BlockSpec((1, 3, D_MODEL), lambda b, j: (b, 0, 0)),
    ] + [_const_spec(a) for a in small]
    return pl.pallas_call(
        _layer_kernel,
        grid=(bsz, seq // tile),
        in_specs=in_specs,
        out_specs=pl.BlockSpec((1, tile, D_MODEL), lambda b, j: (b, j, 0)),
        out_shape=jax.ShapeDtypeStruct((bsz, seq, D_MODEL), F32),
        scratch_shapes=[
            pltpu.VMEM((tile, D_INP), F32),
            pltpu.VMEM((tile, D_MIXP), BF16),
            pltpu.VMEM((tile, A_V + B_VP), F32),
            pltpu.VMEM((A_V, A_QK), F32),
            pltpu.VMEM((B_VP, B_QKP), F32),
            pltpu.VMEM((C_HEADS, C_DK, C_DV), F32),
            pltpu.VMEM((SUBLANES, C_QKV), F32),
        ],
        compiler_params=pltpu.CompilerParams(
            dimension_semantics=("arbitrary", "arbitrary"),
            vmem_limit_bytes=VMEM_LIMIT_BYTES),
        name="hybrid_layer",
    )(x, mod, *small)


def _mod_kernel(c_ref, w_ref, b_ref, o_ref):
    c = c_ref[...]
    c_act = c * _sigmoid(c)
    o_ref[0] = _dot3(c_act, w_ref[0]) + b_ref[0]


def _ada_mod_call(c, ada_w, ada_b):
    depth, d, d3 = ada_w.shape
    bsz = c.shape[0]
    ncol = 512
    assert d3 % ncol == 0
    return pl.pallas_call(
        _mod_kernel,
        grid=(depth, d3 // ncol),
        in_specs=[
            pl.BlockSpec((bsz, d), lambda l, n: (0, 0)),
            pl.BlockSpec((1, d, ncol), lambda l, n: (l, 0, n)),
            pl.BlockSpec((1, 1, ncol), lambda l, n: (l, 0, n)),
        ],
        out_specs=pl.BlockSpec((1, bsz, ncol), lambda l, n: (l, 0, n)),
        out_shape=jax.ShapeDtypeStruct((depth, bsz, d3), F32),
        compiler_params=pltpu.CompilerParams(dimension_semantics=("arbitrary", "arbitrary")),
        name="ada_mod",
    )(c, ada_w, ada_b.reshape(depth, 1, d3))


def _lb_kernel(logit_ref, o_ref):
    depth = logit_ref.shape[0]
    rows = [logit_ref[l:l + 1, :] for l in range(depth)]
    mx = functools.reduce(jnp.maximum, rows)
    ex = [jnp.exp(r - mx) for r in rows]
    inv = 1.0 / functools.reduce(lambda a, b: a + b, ex)
    p = [e * inv for e in ex]
    run = jnp.zeros_like(p[0])
    for l in range(depth):
        run = run + p[l]
        o_ref[l:l + 1, :] = run - p[0]


def _lb_table_call(logits):
    return pl.pallas_call(
        _lb_kernel,
        out_shape=jax.ShapeDtypeStruct(logits.shape, F32),
        name="hgrn_lb",
    )(logits)


def kernel(x, c, w_in, w_out, ada_w, ada_b, ln_g, ln_b, hgrn_lb_logits, gla_w_gk, gla_b_gk, gdn_conv_w, gdn_a_log,
           gdn_dt_bias, gain_a, gain_b, gain_c):
    depth = w_in.shape[0]
    bsz = x.shape[0]
    consts = _constants()
    assert (_padded_mix_sources() == np.arange(D_MIXP)).all() and (B_DKP, B_DVP) == (B_DK, B_DV)

    mod_all = _ada_mod_call(c, ada_w, ada_b).reshape(depth, bsz, 3, D_MODEL)
    lb_table = _lb_table_call(hgrn_lb_logits.astype(F32))
    w_in_all = _gather_padded(w_in, _padded_column_sources(), 2).astype(BF16)

    for l in range(depth):
        w_in_p = w_in_all[l]
        w_out_p = w_out[l].astype(BF16)
        wgk = jnp.zeros((LANES, B_QKP), F32).at[MISC_LR:MISC_LR + GLA_RANK].set(gla_w_gk[l].astype(F32))
        wgk_hi = wgk.astype(BF16)
        wgk_lo = (wgk - wgk_hi.astype(F32)).astype(BF16)
        bgk = gla_b_gk[l].astype(F32).reshape(1, B_QKP)
        avec = jnp.zeros((1, LANES), F32).at[0, MISC_AC:MISC_AC + C_HEADS].set(gdn_a_log[l].astype(F32))
        dtvec = jnp.zeros((1, LANES), F32).at[0, MISC_AC:MISC_AC + C_HEADS].set(gdn_dt_bias[l].astype(F32))
        gain_b_p = jnp.zeros((B_DVP,), F32).at[:B_DV].set(gain_b[l].astype(F32))
        gains = jnp.concatenate([jnp.tile(gain_a[l].astype(F32), A_HEADS), jnp.tile(gain_b_p, B_HEADS),
                                 jnp.tile(gain_c[l].astype(F32), C_HEADS)]).reshape(1, D_MIXP)
        x = _hybrid_layer_call(
            x, mod_all[l], w_in_p, w_out_p, ln_g[l].reshape(1, D_MODEL), ln_b[l].reshape(1, D_MODEL),
            lb_table[l].reshape(1, A_QK), wgk_hi, wgk_lo, bgk, gdn_conv_w[l].astype(F32), avec, dtvec, gains, consts)
    return x
```

```python
import functools

import numpy as np
import jax
import jax.numpy as jnp
from jax import lax
from jax.experimental import pallas as pl
from jax.experimental.pallas import tpu as pltpu

F32 = jnp.float32
BF16 = jnp.bfloat16

D_MODEL = 1024
DEPTH = 2
CHUNK = 64
CONV_WIDTH = 4
assert CONV_WIDTH == 4
A_HEADS, A_DK, A_DV = 4, 64, 64
B_HEADS, B_DK, B_DV = 4, 48, 96
GLA_RANK = 16
GLA_GATE_TEMP = 16.0
C_HEADS, C_DK, C_DV = 6, 64, 64
DEEPNORM_ALPHA = (2 * DEPTH) ** 0.25
LN_EPS = 1e-5
NORM_EPS = 1e-6
FORGET_FLOOR = 1e-30

A_QK = A_HEADS * A_DK
A_V = A_HEADS * A_DV
B_QK = B_HEADS * B_DK
B_V = B_HEADS * B_DV
C_QK = C_HEADS * C_DK
C_V = C_HEADS * C_DV
C_QKV = 2 * C_QK + C_V
SPLIT_SIZES = (A_QK, A_QK, A_V, A_V, B_QK, B_QK, B_V, GLA_RANK, B_V, C_QKV, C_HEADS, C_HEADS, C_V)
D_IN = sum(SPLIT_SIZES)

LANES = 128
SUBLANES = 8
VMEM_LIMIT_BYTES = 56 * 1024 * 1024

B_DKP = B_DK
B_DVP = B_DV
B_QKP = B_HEADS * B_DKP
B_VP = B_HEADS * B_DVP
B_QK_SLOT = 256
SUB = 16
NSUB = CHUNK // SUB
PACK = 16
FSUB = 32
DECAY_CAP = 60.0
CHUNKS_PER_STEP = 4

OFF_QA = 0
OFF_FA = OFF_QA + A_QK
OFF_IA = OFF_FA + A_QK
OFF_ZA = OFF_IA + A_V
OFF_QB = OFF_ZA + A_V
OFF_KB = OFF_QB + B_QK_SLOT
OFF_VB = OFF_KB + B_QK_SLOT
OFF_ZB = OFF_VB + B_VP
OFF_QKVC = OFF_ZB + B_VP
OFF_ZC = OFF_QKVC + C_QKV
D_INP = OFF_ZC + C_V
OFF_MISC = OFF_QB + B_QK_SLOT - LANES
MISC_LR = B_QKP - (B_QK_SLOT - LANES)
MISC_AC = MISC_LR + GLA_RANK
MISC_BC = MISC_AC + C_HEADS
assert MISC_BC + C_HEADS <= LANES and D_INP % LANES == 0
D_MIXP = A_V + B_VP + C_V
OFF_YA, OFF_YB, OFF_YC = 0, A_V, A_V + B_VP


def _padded_column_sources():
    offs = np.concatenate([[0], np.cumsum(SPLIT_SIZES)])
    (o_qa, o_fa, o_ia, o_za, o_qb, o_kb, o_vb, o_lr, o_zb, o_qkvc, o_ac, o_bc, o_zc) = offs[:-1]
    src = np.full((D_INP,), -1, np.int64)
    for dst, s0 in ((OFF_QA, o_qa), (OFF_FA, o_fa), (OFF_IA, o_ia), (OFF_ZA, o_za)):
        src[dst:dst + A_QK] = np.arange(s0, s0 + A_QK)
    for h in range(B_HEADS):
        src[OFF_QB + h * B_DKP:OFF_QB + h * B_DKP + B_DK] = np.arange(o_qb + h * B_DK, o_qb + (h + 1) * B_DK)
        src[OFF_KB + h * B_DKP:OFF_KB + h * B_DKP + B_DK] = np.arange(o_kb + h * B_DK, o_kb + (h + 1) * B_DK)
        src[OFF_VB + h * B_DVP:OFF_VB + h * B_DVP + B_DV] = np.arange(o_vb + h * B_DV, o_vb + (h + 1) * B_DV)
        src[OFF_ZB + h * B_DVP:OFF_ZB + h * B_DVP + B_DV] = np.arange(o_zb + h * B_DV, o_zb + (h + 1) * B_DV)
    src[OFF_QKVC:OFF_QKVC + C_QKV] = np.arange(o_qkvc, o_qkvc + C_QKV)
    src[OFF_ZC:OFF_ZC + C_V] = np.arange(o_zc, o_zc + C_V)
    src[OFF_MISC + MISC_LR:OFF_MISC + MISC_LR + GLA_RANK] = np.arange(o_lr, o_lr + GLA_RANK)
    src[OFF_MISC + MISC_AC:OFF_MISC + MISC_AC + C_HEADS] = np.arange(o_ac, o_ac + C_HEADS)
    src[OFF_MISC + MISC_BC:OFF_MISC + MISC_BC + C_HEADS] = np.arange(o_bc, o_bc + C_HEADS)
    return src


def _padded_mix_sources():
    src = np.full((D_MIXP,), -1, np.int64)
    src[OFF_YA:OFF_YA + A_V] = np.arange(0, A_V)
    for h in range(B_HEADS):
        src[OFF_YB + h * B_DVP:OFF_YB + h * B_DVP + B_DV] = np.arange(A_V + h * B_DV, A_V + (h + 1) * B_DV)
    src[OFF_YC:OFF_YC + C_V] = np.arange(A_V + B_V, A_V + B_V + C_V)
    return src


def _gather_padded(w, src, axis):
    idx = jnp.asarray(np.maximum(src, 0), jnp.int32)
    valid = jnp.asarray(src >= 0)
    shape = [1] * w.ndim
    shape[axis] = src.shape[0]
    return jnp.where(valid.reshape(shape), jnp.take(w, idx, axis=axis), 0.0)


def _block_ones(rows, rgroup, cols, cgroup):
    r = np.arange(rows)[:, None] // rgroup
    c = np.arange(cols)[None, :] // cgroup
    return (r == c).astype(np.float32)


def _head_select(nheads, width, group):
    m = np.zeros((max(nheads, SUBLANES), width), np.float32)
    for h in range(nheads):
        m[h, h * group:(h + 1) * group] = 1.0
    return m


def _level_masks():
    t = np.arange(CHUNK)[:, None]
    s = np.arange(CHUNK)[None, :]
    masks = []
    m = 1
    while m < CHUNK:
        masks.append(((t // (2 * m) == s // (2 * m)) & (t % (2 * m) >= m) & (s % (2 * m) < m)).astype(np.float32))
        m *= 2
    return np.stack(masks)


def _packed_masks():
    lvl = _level_masks()
    nlev = int(np.log2(PACK))
    lvlp = np.stack([np.tile(lvl[lev][0:PACK, 0:PACK], (1, CHUNK // PACK)) for lev in range(nlev)])
    lane_block = np.arange(CHUNK)[None, :] // PACK
    pick = np.stack([np.broadcast_to(lane_block == b, (PACK, CHUNK)) for b in range(CHUNK // PACK)])
    bdiag = (np.arange(CHUNK)[:, None] // PACK == lane_block)
    return lvlp.astype(np.float32), pick.astype(np.float32), bdiag.astype(np.float32)


def _constants():
    lvlp, pick, bdiag = _packed_masks()
    ltri = np.tril(np.ones((CHUNK, CHUNK), np.float32))
    e_a = np.zeros((LANES, C_QK), np.float32)
    e_b = np.zeros((LANES, C_QK), np.float32)
    for h in range(C_HEADS):
        e_a[MISC_AC + h, h * C_DK:(h + 1) * C_DK] = 1.0
        e_b[MISC_BC + h, h * C_DK:(h + 1) * C_DK] = 1.0
    return dict(
        ltri=jnp.asarray(ltri, BF16),
        ltri_g=jnp.asarray(np.kron(np.eye(CHUNKS_PER_STEP, dtype=np.float32), ltri), BF16),
        e_a=jnp.asarray(e_a, BF16),
        e_b=jnp.asarray(e_b, BF16),
        ones_a=jnp.asarray(_block_ones(A_QK, A_DK, A_V, A_DV), BF16),
        ones_b=jnp.asarray(_block_ones(B_QKP, B_DKP, B_VP, B_DVP), BF16),
        ones_bv=jnp.asarray(_block_ones(B_VP, B_DVP, B_VP, B_DVP), BF16),
        ones_c=jnp.asarray(_block_ones(C_QK, C_DK, C_QK, C_DK), BF16),
        mask_a=jnp.asarray(_block_ones(A_V, A_DV, A_QK, A_DK), F32),
        mask_b=jnp.asarray(_block_ones(B_VP, B_DVP, B_QKP, B_DKP), F32),
        hmk=jnp.asarray(_head_select(A_HEADS, A_QK, A_DK), F32),
        hmk_b=jnp.asarray(_head_select(B_HEADS, B_QKP, B_DKP), F32),
        hmv_a=jnp.asarray(_head_select(A_HEADS, A_V, A_DV), F32),
        hmv_b=jnp.asarray(_head_select(B_HEADS, B_VP, B_DVP), F32),
        lvl=jnp.asarray(_level_masks(), BF16),
        lvlp=jnp.asarray(lvlp, BF16),
        pick=jnp.asarray(pick, BF16),
        bdiag=jnp.asarray(bdiag, BF16),
    )


def _sigmoid(x):
    return 1.0 / (1.0 + jnp.exp(-x))


def _softplus(x):
    return jnp.maximum(x, 0.0) + jnp.log1p(jnp.exp(-jnp.abs(x)))


def _split_bf16(a):
    hi = a.astype(BF16)
    lo = (a - hi.astype(F32)).astype(BF16)
    return hi, lo


def _dot(a, b):
    return jnp.dot(a, b, preferred_element_type=F32)


def _dot_nt(a, b):
    return lax.dot_general(a, b, (((1,), (1,)), ((), ())), preferred_element_type=F32)


def _dot_tn(a, b):
    return lax.dot_general(a, b, (((0,), (0,)), ((), ())), preferred_element_type=F32)


def _dot_lhs2(a, b_bf16):
    hi, lo = _split_bf16(a)
    both = _dot(jnp.concatenate([hi, lo], axis=0), b_bf16)
    return both[0:a.shape[0]] + both[a.shape[0]:]


def _dot_rhs2(a_bf16, b):
    hi, lo = _split_bf16(b)
    return _dot(a_bf16, hi) + _dot(a_bf16, lo)


def _dot3(a, b):
    ah, al = _split_bf16(a)
    bh, bl = _split_bf16(b)
    return _dot(ah, bh) + (_dot(ah, bl) + _dot(al, bh))


def _layer_norm(x):
    mu = jnp.mean(x, axis=-1, keepdims=True)
    xc = x - mu
    var = jnp.mean(xc * xc, axis=-1, keepdims=True)
    return xc * lax.rsqrt(var + LN_EPS)


def _block_decay_span(bcum):
    spans = [bcum[n:n + 1] - bcum[n + FSUB - 1:n + FSUB] for n in range(0, CHUNK, FSUB)]
    return functools.reduce(jnp.maximum, spans)


def _gla_intra_bounded(items):
    pairs = [(c, i) for c in range(len(items)) for i in range(CHUNK // FSUB)]
    qstacks, kscs = [], []
    for c, i in pairs:
        q, k, _, bcum, _, nheads, hmk = items[c]
        n, m = i * FSUB, (i + 1) * FSUB
        ri = bcum[n:n + 1]
        qt = q[n:m] * jnp.exp(bcum[n:m] - ri)
        kscs.append((k[0:m] * jnp.exp(ri - bcum[0:m])).astype(BF16))
        qt16 = qt.astype(BF16)
        hmk16 = hmk.astype(BF16)
        qstacks.append(jnp.concatenate([qt16 * hmk16[h:h + 1] for h in range(nheads)], axis=0))
        yield
    ssts = [_dot_nt(qs, ks) for qs, ks in zip(qstacks, kscs)]
    yield
    masked = []
    for (c, i), sst in zip(pairs, ssts):
        t_io = lax.broadcasted_iota(jnp.int32, sst.shape, 0) % FSUB + i * FSUB
        s_io = lax.broadcasted_iota(jnp.int32, sst.shape, 1)
        masked.append(jnp.where(t_io >= s_io, sst, 0.0).astype(BF16))
    rs = [_dot(ms, items[c][2][0:(i + 1) * FSUB]) for (c, i), ms in zip(pairs, masked)]
    yield
    blocks = []
    for (c, i), r in zip(pairs, rs):
        hmv, nheads = items[c][4], items[c][5]
        o_i = r[0:FSUB] * hmv[0:1]
        for h in range(1, nheads):
            o_i = o_i + r[h * FSUB:(h + 1) * FSUB] * hmv[h:h + 1]
        blocks.append(o_i)
        yield
    per = CHUNK // FSUB
    return [jnp.concatenate(blocks[c * per:(c + 1) * per], axis=0) for c in range(len(items))]


def _gla_intra_direct(q, k, v, v16, bcum, ones_kv, hmk, hmv, nheads):
    tio = lax.broadcasted_iota(jnp.int32, (SUB, q.shape[1]), 0)
    outs = []
    for i in range(NSUB):
        n = i * SUB
        qi = q[n:n + SUB]
        ki = k[n:n + SUB]
        bi = bcum[n:n + SUB]
        vi = v[n:n + SUB]
        xs = []
        for s in range(SUB):
            rel = jnp.minimum(bi - bi[s:s + 1], 0.0)
            x = qi * ki[s:s + 1] * jnp.exp(rel)
            xs.append(jnp.where(tio >= s, x, 0.0))
        x_all = jnp.concatenate(xs, axis=0).astype(BF16)
        r_all = _dot(x_all, ones_kv)
        o_i = r_all[0:SUB] * vi[0:1]
        for s in range(1, SUB):
            o_i = o_i + r_all[s * SUB:(s + 1) * SUB] * vi[s:s + 1]
        if i > 0:
            ri = bcum[n:n + 1]
            ksc = (k[0:n] * jnp.exp(ri - bcum[0:n])).astype(BF16)
            qt = qi * jnp.exp(bi - ri)
            qstack = jnp.concatenate([qt * hmk[h:h + 1] for h in range(nheads)], axis=0).astype(BF16)
            sst = _dot_nt(qstack, ksc)
            r_od = _dot(sst.astype(BF16), v16[0:n])
            for h in range(nheads):
                o_i = o_i + r_od[h * SUB:(h + 1) * SUB] * hmv[h:h + 1]
        outs.append(o_i)
    return jnp.concatenate(outs, axis=0)


def _gla_streams(streams):
    flat = [(si, c) for si, st in enumerate(streams) for c in range(len(st["chunks"]))]
    pre = []
    for si, c in flat:
        q, k, v, bcum = streams[si]["chunks"][c]
        b_last = bcum[CHUNK - 1:CHUNK, :]
        pre.append(dict(v16=v.astype(BF16), qe16=(q * jnp.exp(bcum)).astype(BF16),
                        kdec16=(k * jnp.exp(b_last - bcum)).astype(BF16), d=jnp.exp(b_last)))
        yield
    upds = [_dot_tn(p["v16"], p["kdec16"]) for p in pre]
    yield
    intra = yield from _gla_intra_bounded(
        [streams[si]["chunks"][c][0:2] + (p["v16"], streams[si]["chunks"][c][3], streams[si]["hmv"],
                                         streams[si]["nheads"], streams[si]["hmk"])
         for (si, c), p in zip(flat, pre)])
    inters = [[] for _ in streams]
    for si, st in enumerate(streams):
        state = st["st_ref"][...]
        for c in range(len(st["chunks"])):
            j = flat.index((si, c))
            inters[si].append(_dot_nt(pre[j]["qe16"], state.astype(BF16)))
            state = state * pre[j]["d"] + upds[j] * st["st_mask"]
            yield
        st["st_ref"][...] = state
    intras = [[intra[flat.index((si, c))] for c in range(len(st["chunks"]))] for si, st in enumerate(streams)]
    return inters, intras


def _unit_lower_inverses(nmats, lvl_ref, lvlp_ref, pick_ref, bdiag_ref):
    r = lax.broadcasted_iota(jnp.int32, (CHUNK, CHUNK), 0)
    c = lax.broadcasted_iota(jnp.int32, (CHUNK, CHUNK), 1)
    eye = jnp.where(r == c, 1.0, 0.0)
    nblk = CHUNK // PACK
    bdiag = bdiag_ref[...]

    def pack(x):
        return functools.reduce(lambda a, b: a + b,
                                [x[b * PACK:(b + 1) * PACK] * pick_ref[b] for b in range(nblk)])

    def unpack(xp):
        return jnp.concatenate([xp] * nblk, axis=0) * bdiag

    negs = [(-n).astype(BF16) for n in nmats]
    fulls = [(eye - n * lvl_ref[0].astype(F32)).astype(BF16) for n in nmats]
    negps = [pack(n16) for n16 in negs]
    tps = [pack(t) for t in fulls]
    for lev in range(1, lvlp_ref.shape[0]):
        pps = [_dot(np16, t).astype(BF16) for np16, t in zip(negps, fulls)]
        yield
        gps = [_dot(tp, unpack(pp)).astype(BF16) for tp, pp in zip(tps, pps)]
        yield
        tps = [tp + gp * lvlp_ref[lev] for tp, gp in zip(tps, gps)]
        fulls = [unpack(tp) for tp in tps]
    m = PACK
    lev = lvlp_ref.shape[0]
    while m < CHUNK:
        rows = [(lo, lo + m) for lo in range(m, CHUNK, 2 * m)]
        keep = [(lo - m, lo) for lo in range(m, CHUNK, 2 * m)]

        def take(x):
            return jnp.concatenate([x[a:b] for a, b in rows], axis=0)

        mask_r = take(lvl_ref[lev])
        zeros = jnp.zeros((m, CHUNK), BF16)
        prs = [_dot(take(n16), t).astype(BF16) for n16, t in zip(negs, fulls)]
        yield
        pembs = [jnp.concatenate([piece for i in range(len(rows)) for piece in (zeros, pr[i * m:(i + 1) * m])],
                                 axis=0) for pr in prs]
        grs = [_dot(take(t), pe).astype(BF16) for t, pe in zip(fulls, pembs)]
        yield
        news = [take(t) + gr * mask_r for t, gr in zip(fulls, grs)]
        fulls = [jnp.concatenate([piece for i, (a, b) in enumerate(keep)
                                  for piece in (t[a:b], new[i * m:(i + 1) * m])], axis=0)
                 for t, new in zip(fulls, news)]
        m *= 2
        lev += 1
    return fulls


PREP_DONE = "prep done"


def _interleave(*gens_and_steps):
    live = list(gens_and_steps)
    while live:
        for entry in list(live):
            gen, steps = entry
            try:
                for _ in range(steps):
                    next(gen)
            except StopIteration:
                live.remove(entry)
        yield


def _layer_kernel(x_ref, mod_ref, win_ref, wout_ref, lng_ref, lnb_ref, lb_ref, wgkh_ref, wgkl_ref, bgk_ref,
                  convw_ref, avec_ref, dtvec_ref, gain_ref,
                  ltri_ref, ltrig_ref, ea_ref, eb_ref, ones_a_ref, ones_b_ref, ones_bv_ref, ones_c_ref,
                  mask_a_ref, mask_b_ref, hmk_ref, hmkb_ref, hmva_ref, hmvb_ref, lvl_ref, lvlp_ref, pick_ref, bdiag_ref,
                  out_ref,
                  proj_ref, y_ref, ointer_ref, sa_ref, sb_ref, sc_ref, conv_ref):
    tile = x_ref.shape[1]

    @pl.when(pl.program_id(1) == 0)
    def _():
        sa_ref[...] = jnp.zeros_like(sa_ref)
        sb_ref[...] = jnp.zeros_like(sb_ref)
        sc_ref[...] = jnp.zeros_like(sc_ref)
        conv_ref[...] = jnp.zeros_like(conv_ref)

    x = x_ref[0]
    shift = mod_ref[0, 0:1, :]
    scale = mod_ref[0, 1:2, :]
    gate = mod_ref[0, 2:3, :]
    h = _layer_norm(x) * (1.0 + scale) + shift
    proj_ref[...] = _dot(h.astype(BF16), win_ref[...])

    ltri = ltri_ref[...]
    hmk = hmk_ref[...]
    hmkb = hmkb_ref[...]
    hmva = hmva_ref[...]
    hmvb = hmvb_ref[...]
    t_io = lax.broadcasted_iota(jnp.int32, (CHUNK, CHUNK), 0)
    s_io = lax.broadcasted_iota(jnp.int32, (CHUNK, CHUNK), 1)
    causal = t_io >= s_io
    strict = t_io > s_io

    def norm_gate(o, z, ones, dv, gain):
        ms = _dot((o * o).astype(BF16), ones) * (1.0 / dv)
        return (o * lax.rsqrt(ms + NORM_EPS) * gain * (z * _sigmoid(z))).astype(BF16)

    fallbacks = []

    def group_gen(g):
        chunk_rows = [pl.ds((g * CHUNKS_PER_STEP + i) * CHUNK, CHUNK) for i in range(CHUNKS_PER_STEP)]
        group_rows = pl.ds(g * (CHUNKS_PER_STEP * CHUNK), CHUNKS_PER_STEP * CHUNK)

        def chunks_of(arr):
            return [arr[i * CHUNK:(i + 1) * CHUNK] for i in range(CHUNKS_PER_STEP)]

        def ab_inputs(rows, ltri_rows):
            qa = proj_ref[rows, OFF_QA:OFF_QA + A_QK]
            fa = proj_ref[rows, OFF_FA:OFF_FA + A_QK]
            lbv = lb_ref[...]
            sg = _sigmoid(fa)
            f_a = lbv + (1.0 - lbv) * sg
            g_a = jnp.log(jnp.maximum(f_a, FORGET_FLOOR))
            k_a = (1.0 - lbv) * (1.0 - sg)
            q_a = qa * _sigmoid(qa)
            misc = proj_ref[rows, OFF_MISC:OFF_MISC + LANES]
            m_hi, m_lo = _split_bf16(misc)
            n = misc.shape[0]
            gk_hi = _dot(jnp.concatenate([m_hi, m_lo], axis=0), wgkh_ref[...])
            gk_lin = gk_hi[0:n] + (gk_hi[n:] + _dot(m_hi, wgkl_ref[...])) + bgk_ref[...]
            g_b = -_softplus(-gk_lin) * (1.0 / GLA_GATE_TEMP)
            in_a = (q_a, k_a, proj_ref[rows, OFF_IA:OFF_IA + A_V], _dot_rhs2(ltri_rows, g_a))
            in_b = (proj_ref[rows, OFF_QB:OFF_QB + B_QKP] * (B_DK ** -0.5), proj_ref[rows, OFF_KB:OFF_KB + B_QKP],
                    proj_ref[rows, OFF_VB:OFF_VB + B_VP], _dot_rhs2(ltri_rows, g_b))
            return in_a, in_b, misc

        def store_y_ab(rows, o_a, o_b):
            za = proj_ref[rows, OFF_ZA:OFF_ZA + A_V]
            y_ref[rows, OFF_YA:OFF_YA + A_V] = norm_gate(o_a, za, ones_a_ref[...], A_DV,
                                                         gain_ref[:, OFF_YA:OFF_YA + A_V])
            zb = proj_ref[rows, OFF_ZB:OFF_ZB + B_VP]
            y_ref[rows, OFF_YB:OFF_YB + B_VP] = norm_gate(o_b, zb, ones_bv_ref[...], B_DV,
                                                          gain_ref[:, OFF_YB:OFF_YB + B_VP])

        in_a, in_b, misc_g = ab_inputs(group_rows, ltrig_ref[...])
        chunks_a = list(zip(*[chunks_of(t) for t in in_a]))
        chunks_b = list(zip(*[chunks_of(t) for t in in_b]))

        def ab_work():
            stream_a = dict(chunks=chunks_a, st_ref=sa_ref, st_mask=mask_a_ref[...], hmk=hmk, hmv=hmva,
                            nheads=A_HEADS)
            stream_b = dict(chunks=chunks_b, st_ref=sb_ref, st_mask=mask_b_ref[...], hmk=hmkb, hmv=hmvb,
                            nheads=B_HEADS)
            (inter_a, inter_b), (intra_a, intra_b) = yield from _gla_streams([stream_a, stream_b])
            inter_a, inter_b = jnp.concatenate(inter_a, axis=0), jnp.concatenate(inter_b, axis=0)
            ointer_ref[group_rows, 0:A_V] = inter_a
            ointer_ref[group_rows, A_V:A_V + B_VP] = inter_b
            yield
            store_y_ab(group_rows, inter_a + jnp.concatenate(intra_a, axis=0),
                       inter_b + jnp.concatenate(intra_b, axis=0))
            yield

        span_a = jnp.max(functools.reduce(jnp.maximum, [_block_decay_span(c[3]) for c in chunks_a]))
        span_b = jnp.max(functools.reduce(jnp.maximum, [_block_decay_span(c[3]) for c in chunks_b]))
        unbounded = jnp.logical_not(jnp.maximum(span_a, span_b) <= DECAY_CAP)
        yield

        ones_c = ones_c_ref[...]
        cw = convw_ref[...]
        glen = CHUNKS_PER_STEP * CHUNK
        raw = proj_ref[group_rows, OFF_QKVC:OFF_QKVC + C_QKV]
        win = jnp.concatenate([conv_ref[...], raw], axis=0)
        conv_ref[...] = raw[glen - SUBLANES:glen]
        back1 = pltpu.roll(win, 1, axis=0)
        older = pltpu.roll(win * cw[1:2] + back1 * cw[0:1], 2, axis=0)
        acc = (win * cw[3:4] + back1 * cw[2:3] + older)[SUBLANES:SUBLANES + glen]
        yield
        qkv = acc * _sigmoid(acc)
        q_c = qkv[:, 0:C_QK]
        k_c = qkv[:, C_QK:2 * C_QK]
        v_c = qkv[:, 2 * C_QK:C_QKV]
        yield
        ssq = _dot(jnp.concatenate([q_c * q_c, k_c * k_c], axis=0).astype(BF16), ones_c)
        q_c = q_c * lax.rsqrt(ssq[0:glen] + NORM_EPS) * (C_DK ** -0.5)
        k_c = k_c * lax.rsqrt(ssq[glen:] + NORM_EPS)
        yield
        log_a = -jnp.exp(avec_ref[...]) * _softplus(misc_g + dtvec_ref[...])
        beta = _sigmoid(misc_g)
        bc_small = _dot_rhs2(ltrig_ref[...], log_a)
        bexp = _dot_lhs2(bc_small, ea_ref[...])
        beta_x = _dot_lhs2(beta, eb_ref[...])
        e_b = jnp.exp(bexp)
        bl_rows = [bexp[(i + 1) * CHUNK - 1:(i + 1) * CHUNK, :] for i in range(CHUNKS_PER_STEP)]
        bl_g = jnp.concatenate([jnp.broadcast_to(bl, (CHUNK, C_QK)) for bl in bl_rows], axis=0)
        yield
        kb_c = k_c * beta_x
        kdl_c = k_c * jnp.exp(bl_g - bexp)
        whole = dict(bexp=bexp, q16=q_c.astype(BF16), k16=k_c.astype(BF16), kb16=kb_c.astype(BF16),
                     vb16=(v_c * beta_x).astype(BF16), kbe16=(kb_c * e_b).astype(BF16),
                     qe16=(q_c * e_b).astype(BF16), kdl16=kdl_c.astype(BF16))
        prep = [dict({name: arr[i * CHUNK:(i + 1) * CHUNK] for name, arr in whole.items()},
                     b_rows=bc_small[i * CHUNK:(i + 1) * CHUNK].T, e_last=jnp.exp(bl_rows[i]))
                for i in range(CHUNKS_PER_STEP)]
        yield

        items = [(i, hd) for i in range(CHUNKS_PER_STEP) for hd in range(C_HEADS)]

        def hsl(hd):
            return slice(hd * C_DK, (hd + 1) * C_DK)

        kqks = [_dot_nt(jnp.concatenate([prep[i]["kb16"][:, hsl(hd)], prep[i]["q16"][:, hsl(hd)]], axis=0),
                        prep[i]["k16"][:, hsl(hd)]) for i, hd in items]
        kks = [kq[0:CHUNK] for kq in kqks]
        qks = [kq[CHUNK:2 * CHUNK] for kq in kqks]
        yield
        decs = [jnp.exp(jnp.minimum(
            prep[i]["bexp"][:, hsl(hd)] - prep[i]["b_rows"][MISC_AC + hd:MISC_AC + hd + 1, :], 0.0))
            for i, hd in items]
        nmats = [jnp.where(strict, kk * dec, 0.0) for kk, dec in zip(kks, decs)]
        attns = [jnp.where(causal, qk * dec, 0.0).astype(BF16) for qk, dec in zip(qks, decs)]
        yield PREP_DONE
        t16s = []

        def inverse_work():
            t16s.extend((yield from _unit_lower_inverses(nmats, lvl_ref, lvlp_ref, pick_ref, bdiag_ref)))

        yield from _interleave((inverse_work(), 1), (ab_work(), 3))
        us = [_dot(t16, prep[i]["vb16"][:, hsl(hd)]) for t16, (i, hd) in zip(t16s, items)]
        ws = [_dot(t16, prep[i]["kbe16"][:, hsl(hd)]).astype(BF16) for t16, (i, hd) in zip(t16s, items)]
        yield
        u16s = [u.astype(BF16) for u in us]
        kws = [_dot_tn(prep[i]["kdl16"][:, hsl(hd)], w).astype(BF16) for w, (i, hd) in zip(ws, items)]
        kus = [_dot_tn(prep[i]["kdl16"][:, hsl(hd)], u16) for u16, (i, hd) in zip(u16s, items)]
        yield

        states = [sc_ref[hd] for hd in range(C_HEADS)]
        s16s = []
        for i in range(CHUNKS_PER_STEP):
            base = i * C_HEADS
            cur16 = [s.astype(BF16) for s in states]
            s16s.extend(cur16)
            moved = [_dot(kws[base + hd], cur16[hd]) for hd in range(C_HEADS)]
            states = [states[hd] * prep[i]["e_last"][:, hsl(hd)] - moved[hd] + kus[base + hd]
                      for hd in range(C_HEADS)]
            yield
        for hd in range(C_HEADS):
            sc_ref[hd] = states[hd]

        ws_os = [_dot(jnp.concatenate([w, prep[i]["qe16"][:, hsl(hd)]], axis=0), s16)
                 for w, s16, (i, hd) in zip(ws, s16s, items)]
        yield
        vn16s = [(u - wo[0:CHUNK]).astype(BF16) for u, wo in zip(us, ws_os)]
        o_items = [wo[CHUNK:2 * CHUNK] + _dot(attn, vn16) for wo, attn, vn16 in zip(ws_os, attns, vn16s)]
        yield
        o_c = jnp.concatenate([jnp.concatenate(o_items[i * C_HEADS:(i + 1) * C_HEADS], axis=1)
                               for i in range(CHUNKS_PER_STEP)], axis=0)
        zc = proj_ref[group_rows, OFF_ZC:OFF_ZC + C_V]
        y_ref[group_rows, OFF_YC:OFF_YC + C_V] = norm_gate(o_c, zc, ones_c, C_DV, gain_ref[:, OFF_YC:OFF_YC + C_V])

        def redo_direct():
            for rows in chunk_rows:
                (q_a, k_a, v_a, bcum_a), (q_b, k_b, v_b, bcum_b), _ = ab_inputs(rows, ltri)
                o_a = _gla_intra_direct(q_a, k_a, v_a, v_a.astype(BF16), bcum_a, ones_a_ref[...], hmk, hmva,
                                        A_HEADS)
                o_b = _gla_intra_direct(q_b, k_b, v_b, v_b.astype(BF16), bcum_b, ones_b_ref[...], hmkb, hmvb,
                                        B_HEADS)
                store_y_ab(rows, ointer_ref[rows, 0:A_V] + o_a, ointer_ref[rows, A_V:A_V + B_VP] + o_b)

        fallbacks.append((unbounded, redo_direct))

    gens = [group_gen(g) for g in range(tile // (CHUNK * CHUNKS_PER_STEP))]

    def run_to_prep_done(gen):
        for token in gen:
            if token is PREP_DONE:
                return

    run_to_prep_done(gens[0])
    for g, gen in enumerate(gens):
        ahead = gens[g + 1] if g + 1 < len(gens) else None
        for _ in gen:
            if ahead is not None and next(ahead) is PREP_DONE:
                ahead = None
        if ahead is not None:
            run_to_prep_done(ahead)
    for unbounded, redo_direct in fallbacks:
        pl.when(unbounded)(redo_direct)

    out = _dot(y_ref[...], wout_ref[...])
    res = DEEPNORM_ALPHA * x + gate * out
    out_ref[0] = _layer_norm(res) * lng_ref[...] + lnb_ref[...]


def _seq_tile(seq):
    tile = CHUNK * CHUNKS_PER_STEP
    assert seq % tile == 0
    while tile * 2 <= min(seq, 512) and seq % (tile * 2) == 0:
        tile *= 2
    return tile


def _const_spec(arr):
    nd = arr.ndim
    return pl.BlockSpec(arr.shape, lambda b, j, _nd=nd: (0,) * _nd)


def _hybrid_layer_call(x, mod, w_in_p, w_out_p, ln_g, ln_b, lbv, wgk_hi, wgk_lo, bgk, conv_w, avec, dtvec, gains,
                       consts):
    bsz, seq, d = x.shape
    assert d == D_MODEL and seq % CHUNK == 0
    tile = _seq_tile(seq)
    names = ("ltri", "ltri_g", "e_a", "e_b", "ones_a", "ones_b", "ones_bv", "ones_c", "mask_a", "mask_b",
             "hmk", "hmk_b", "hmv_a", "hmv_b", "lvl", "lvlp", "pick", "bdiag")
    small = (w_in_p, w_out_p, ln_g, ln_b, lbv, wgk_hi, wgk_lo, bgk, conv_w, avec, dtvec, gains) + tuple(
        consts[n] for n in names)
    in_specs = [
        pl.BlockSpec((1, tile, D_MODEL), lambda b, j: (b, j, 0)),
        pl.BlockSpec((1, 3, D_MODEL), lambda b, j: (b, 0, 0)),
    ] + [_const_spec(a) for a in small]
    return pl.pallas_call(
        _layer_kernel,
        grid=(bsz, seq // tile),
        in_specs=in_specs,
        out_specs=pl.BlockSpec((1, tile, D_MODEL), lambda b, j: (b, j, 0)),
        out_shape=jax.ShapeDtypeStruct((bsz, seq, D_MODEL), F32),
        scratch_shapes=[
            pltpu.VMEM((tile, D_INP), F32),
            pltpu.VMEM((tile, D_MIXP), BF16),
            pltpu.VMEM((tile, A_V + B_VP), F32),
            pltpu.VMEM((A_V, A_QK), F32),
            pltpu.VMEM((B_VP, B_QKP), F32),
            pltpu.VMEM((C_HEADS, C_DK, C_DV), F32),
            pltpu.VMEM((SUBLANES, C_QKV), F32),
        ],
        compiler_params=pltpu.CompilerParams(
            dimension_semantics=("arbitrary", "arbitrary"),
            vmem_limit_bytes=VMEM_LIMIT_BYTES),
        name="hybrid_layer",
    )(x, mod, *small)


def _mod_kernel(c_ref, w_ref, b_ref, o_ref):
    c = c_ref[...]
    c_act = c * _sigmoid(c)
    o_ref[0] = _dot3(c_act, w_ref[0]) + b_ref[0]


def _ada_mod_call(c, ada_w, ada_b):
    depth, d, d3 = ada_w.shape
    bsz = c.shape[0]
    ncol = 512
    assert d3 % ncol == 0
    return pl.pallas_call(
        _mod_kernel,
        grid=(depth, d3 // ncol),
        in_specs=[
            pl.BlockSpec((bsz, d), lambda l, n: (0, 0)),
            pl.BlockSpec((1, d, ncol), lambda l, n: (l, 0, n)),
            pl.BlockSpec((1, 1, ncol), lambda l, n: (l, 0, n)),
        ],
        out_specs=pl.BlockSpec((1, bsz, ncol), lambda l, n: (l, 0, n)),
        out_shape=jax.ShapeDtypeStruct((depth, bsz, d3), F32),
        compiler_params=pltpu.CompilerParams(dimension_semantics=("arbitrary", "arbitrary")),
        name="ada_mod",
    )(c, ada_w, ada_b.reshape(depth, 1, d3))


def _lb_kernel(logit_ref, o_ref):
    depth = logit_ref.shape[0]
    rows = [logit_ref[l:l + 1, :] for l in range(depth)]
    mx = functools.reduce(jnp.maximum, rows)
    ex = [jnp.exp(r - mx) for r in rows]
    inv = 1.0 / functools.reduce(lambda a, b: a + b, ex)
    p = [e * inv for e in ex]
    run = jnp.zeros_like(p[0])
    for l in range(depth):
        run = run + p[l]
        o_ref[l:l + 1, :] = run - p[0]


def _lb_table_call(logits):
    return pl.pallas_call(
        _lb_kernel,
        out_shape=jax.ShapeDtypeStruct(logits.shape, F32),
        name="hgrn_lb",
    )(logits)


def kernel(x, c, w_in, w_out, ada_w, ada_b, ln_g, ln_b, hgrn_lb_logits, gla_w_gk, gla_b_gk, gdn_conv_w, gdn_a_log,
           gdn_dt_bias, gain_a, gain_b, gain_c):
    depth = w_in.shape[0]
    bsz = x.shape[0]
    consts = _constants()
    assert (_padded_mix_sources() == np.arange(D_MIXP)).all() and (B_DKP, B_DVP) == (B_DK, B_DV)

    mod_all = _ada_mod_call(c, ada_w, ada_b).reshape(depth, bsz, 3, D_MODEL)
    lb_table = _lb_table_call(hgrn_lb_logits.astype(F32))
    w_in_all = _gather_padded(w_in, _padded_column_sources(), 2).astype(BF16)

    for l in range(depth):
        w_in_p = w_in_all[l]
        w_out_p = w_out[l].astype(BF16)
        wgk = jnp.zeros((LANES, B_QKP), F32).at[MISC_LR:MISC_LR + GLA_RANK].set(gla_w_gk[l].astype(F32))
        wgk_hi = wgk.astype(BF16)
        wgk_lo = (wgk - wgk_hi.astype(F32)).astype(BF16)
        bgk = gla_b_gk[l].astype(F32).reshape(1, B_QKP)
        avec = jnp.zeros((1, LANES), F32).at[0, MISC_AC:MISC_AC + C_HEADS].set(gdn_a_log[l].astype(F32))
        dtvec = jnp.zeros((1, LANES), F32).at[0, MISC_AC:MISC_AC + C_HEADS].set(gdn_dt_bias[l].astype(F32))
        gain_b_p = jnp.zeros((B_DVP,), F32).at[:B_DV].set(gain_b[l].astype(F32))
        gains = jnp.concatenate([jnp.tile(gain_a[l].astype(F32), A_HEADS), jnp.tile(gain_b_p, B_HEADS),
                                 jnp.tile(gain_c[l].astype(F32), C_HEADS)]).reshape(1, D_MIXP)
        x = _hybrid_layer_call(
            x, mod_all[l], w_in_p, w_out_p, ln_g[l].reshape(1, D_MODEL), ln_b[l].reshape(1, D_MODEL),
            lb_table[l].reshape(1, A_QK), wgk_hi, wgk_lo, bgk, gdn_conv_w[l].astype(F32), avec, dtvec, gains, consts)
    return x
```
